```python
import jax
import jax.numpy as jnp
from jax import lax
import numpy as np

D_MODEL = 1024
BATCH = 2
SEQ = 8192
DEPTH = 2

N_BRANCH = 3
A_WIDTH = D_MODEL
A_GROUPS = 4
A_CHUNK = 128
B_HEADS = 8
B_HEAD_DIM = 64
B_WIDTH = B_HEADS * B_HEAD_DIM
Q_BLOCK = 128
C_HEADS = 8
C_HEAD_DIM = 64
C_WIDTH = C_HEADS * C_HEAD_DIM
C_DECAY_LORA = 64
C_AAA_LORA = 64
C_MV_LORA = 32
C_GATE_LORA = 128
C_GN_EPS = 64e-5
D_FF = ((8 * D_MODEL // 3 + 255) // 256) * 256

A_COLS = 2 * A_WIDTH
B_COLS = 3 * B_WIDTH + B_HEADS
C_COLS = 3 * C_WIDTH + C_DECAY_LORA + C_AAA_LORA + C_GATE_LORA
G_COLS = N_BRANCH * D_MODEL
IN_COLS = A_COLS + B_COLS + C_COLS + G_COLS
IN_SPLITS = (A_COLS, A_COLS + B_COLS, A_COLS + B_COLS + C_COLS)
B_SPLITS = (B_WIDTH, 2 * B_WIDTH, 3 * B_WIDTH)
C_SPLITS = (C_WIDTH, 2 * C_WIDTH, 3 * C_WIDTH, 3 * C_WIDTH + C_DECAY_LORA,
            3 * C_WIDTH + C_DECAY_LORA + C_AAA_LORA)
NORM_EPS = 1e-6
LN_EPS = 1e-5

kernel_name = 'hybrid_gated_gmlp_fox_rwkv7'


def rmsnorm(x, g):
    xf = x.astype(jnp.float32)
    y = xf * lax.rsqrt(jnp.mean(xf * xf, axis=-1, keepdims=True) + NORM_EPS)
    return (y * g.astype(jnp.float32)).astype(x.dtype)


def layernorm(x, g, b):
    xf = x.astype(jnp.float32)
    mu = jnp.mean(xf, axis=-1, keepdims=True)
    var = jnp.mean(jnp.square(xf - mu), axis=-1, keepdims=True)
    y = (xf - mu) * lax.rsqrt(var + LN_EPS)
    return (y * g.astype(jnp.float32) + b.astype(jnp.float32)).astype(x.dtype)


def token_shift(p):
    return jnp.pad(p, ((0, 0), (1, 0), (0, 0)))[:, :-1]


def spatial_gating_unit(u, v, ln_g, ln_b, w_s, b_s):
    bn, s, _ = v.shape
    n_chunks = s // A_CHUNK
    gd = A_WIDTH // A_GROUPS
    v = layernorm(v, ln_g, ln_b)
    causal = jnp.tril(jnp.ones((A_CHUNK, A_CHUNK), dtype=bool))
    w = jnp.where(causal[None], w_s, jnp.zeros_like(w_s)).astype(v.dtype)
    vc = v.reshape(bn, n_chunks, A_CHUNK, A_GROUPS, gd)
    mixed = jnp.einsum('gts,bcsgd->bctgd', w, vc) + b_s.T.astype(v.dtype)[None, None, :, :, None]
    return u * mixed.reshape(bn, s, A_WIDTH)


def forgetting_attention(q, k, v, f_logit, b_f):
    bn, s = q.shape[:2]
    n_blocks = s // Q_BLOCK
    log_f = jax.nn.log_sigmoid((f_logit + b_f).astype(jnp.float32))
    c = jnp.cumsum(log_f, axis=1)
    c_keys = c.transpose(0, 2, 1)
    kf = k.astype(jnp.float32)
    vf = v.astype(jnp.float32)
    key_pos = jnp.arange(s)
    qb = (q.astype(jnp.float32) * B_HEAD_DIM ** -0.5).reshape(
        bn, n_blocks, Q_BLOCK, B_HEADS, B_HEAD_DIM).swapaxes(0, 1)
    cb = c.reshape(bn, n_blocks, Q_BLOCK, B_HEADS).swapaxes(0, 1)

    def query_block(args):
        i, q_i, c_i = args
        logits = jnp.einsum('bqhd,bkhd->bhqk', q_i, kf)
        logits = logits + (c_i.transpose(0, 2, 1)[..., :, None] - c_keys[:, :, None, :])
        q_pos = i * Q_BLOCK + jnp.arange(Q_BLOCK)
        mask = key_pos[None, :] <= q_pos[:, None]
        logits = jnp.where(mask, logits, -jnp.inf)
        p = jax.nn.softmax(logits, axis=-1)
        return jnp.einsum('bhqk,bkhd->bqhd', p, vf)

    out = lax.map(query_block, (jnp.arange(n_blocks), qb, cb))
    return out.swapaxes(0, 1).reshape(bn, s, B_WIDTH).astype(q.dtype)


def rwkv7_time_mix(r, k, v, w_lo, a_lo, g_lo, w0, w_up, a0, a_up, g_up,
                   k_k, k_a, r_k, lnx_g, lnx_b):
    bn, s, _ = r.shape
    f32 = jnp.float32
    w = -jax.nn.softplus(-(w0 + jnp.tanh(w_lo) @ w_up).astype(f32)) - 0.5
    decay = jnp.exp(-jnp.exp(w))
    a = jax.nn.sigmoid((a0 + a_lo @ a_up).astype(f32))
    g = (jax.nn.sigmoid(g_lo) @ g_up).astype(f32)
    kk = (k * k_k).astype(f32).reshape(bn, s, C_HEADS, C_HEAD_DIM)
    kk = kk / jnp.maximum(jnp.sqrt(jnp.sum(kk * kk, axis=-1, keepdims=True)), 1e-12)
    k = k.astype(f32) * (1.0 + (a - 1.0) * k_a.astype(f32))

    def heads(t):
        return t.astype(f32).reshape(bn, s, C_HEADS, C_HEAD_DIM)

    rh, kh, vh, ah, dh = heads(r), heads(k), heads(v), heads(a), heads(decay)

    def step(state, inp):
        r_t, d_t, k_t, v_t, kk_t, a_t = inp
        sa = jnp.einsum('bhij,bhj->bhi', state, kk_t)
        state = (state * d_t[..., None, :]
                 - sa[..., :, None] * (kk_t * a_t)[..., None, :]
                 + v_t[..., :, None] * k_t[..., None, :])
        y_t = jnp.einsum('bhij,bhj->bhi', state, r_t)
        return state, y_t

    tm = lambda t: t.swapaxes(0, 1)
    s0 = jnp.zeros((bn, C_HEADS, C_HEAD_DIM, C_HEAD_DIM), f32)
    _, y = lax.scan(step, s0, (tm(rh), tm(dh), tm(kh), tm(vh), tm(kk), tm(ah)))
    y = y.swapaxes(0, 1)
    mu = jnp.mean(y, axis=-1, keepdims=True)
    var = jnp.mean(jnp.square(y - mu), axis=-1, keepdims=True)
    y = ((y - mu) * lax.rsqrt(var + C_GN_EPS)).reshape(bn, s, C_WIDTH)
    y = y * lnx_g.astype(f32) + lnx_b.astype(f32)
    bonus = jnp.sum(rh * kh * r_k.astype(f32), axis=-1, keepdims=True) * vh
    y = y + bonus.reshape(bn, s, C_WIDTH)
    return (y * g).astype(r.dtype)


def setup_inputs(seed: int = 0) -> dict:
    key = jax.random.key(seed)
    keys = list(jax.random.split(key, 40))
    f32 = jnp.float32

    def nrm(shape, scale):
        return scale * jax.random.normal(keys.pop(), shape, f32)

    def unif(shape, lo, hi):
        return jax.random.uniform(keys.pop(), shape, f32, lo, hi)

    L = DEPTH
    Lv = DEPTH - 1
    return {
        'x': nrm((BATCH, SEQ, D_MODEL), 1.0),
        'norm_mix': 1.0 + nrm((L, D_MODEL), 0.02),
        'w_in': nrm((L, D_MODEL, IN_COLS), D_MODEL ** -0.5),
        'gate_bias': nrm((L, N_BRANCH, D_MODEL), 0.02),
        'a_ln_g': 1.0 + nrm((L, A_WIDTH), 0.02),
        'a_ln_b': nrm((L, A_WIDTH), 0.02),
        'a_w_s': nrm((L, A_GROUPS, A_CHUNK, A_CHUNK), A_CHUNK ** -0.5),
        'a_b_s': 1.0 + nrm((L, A_GROUPS, A_CHUNK), 0.1),
        'b_f_bias': unif((L, B_HEADS), 1.0, 5.0),
        'c_mu': unif((L, C_COLS), 0.0, 1.0),
        'c_w0': unif((L, C_WIDTH), -3.0, 1.0),
        'c_w_up': nrm((L, C_DECAY_LORA, C_WIDTH), 0.1),
        'c_a0': nrm((L, C_WIDTH), 0.1),
        'c_a_up': nrm((L, C_AAA_LORA, C_WIDTH), C_AAA_LORA ** -0.5),
        'c_g_up': nrm((L, C_GATE_LORA, C_WIDTH), C_GATE_LORA ** -0.5),
        'c_k_k': 0.85 + nrm((L, C_WIDTH), 0.02),
        'c_k_a': 1.0 + nrm((L, C_WIDTH), 0.02),
        'c_r_k': nrm((L, C_HEADS, C_HEAD_DIM), 0.1),
        'c_lnx_g': 1.0 + nrm((L, C_WIDTH), 0.02),
        'c_lnx_b': nrm((L, C_WIDTH), 0.02),
        'c_v0': nrm((Lv, C_WIDTH), 0.1),
        'c_v_down': nrm((Lv, C_WIDTH, C_MV_LORA), C_WIDTH ** -0.5),
        'c_v_up': nrm((Lv, C_MV_LORA, C_WIDTH), C_MV_LORA ** -0.5),
        'p_a': nrm((L, A_WIDTH, D_MODEL), A_WIDTH ** -0.5),
        'p_b': nrm((L, B_WIDTH, D_MODEL), B_WIDTH ** -0.5),
        'p_c': nrm((L, C_WIDTH, D_MODEL), C_WIDTH ** -0.5),
        'w_out': nrm((L, D_MODEL, D_MODEL), D_MODEL ** -0.5),
        'norm_ffn': 1.0 + nrm((L, D_MODEL), 0.02),
        'w_gate_up': nrm((L, D_MODEL, 2 * D_FF), D_MODEL ** -0.5),
        'w_down': nrm((L, D_FF, D_MODEL), D_FF ** -0.5),
        'norm_final': 1.0 + nrm((D_MODEL,), 0.02),
    }


def reference(x, norm_mix, w_in, gate_bias, a_ln_g, a_ln_b, a_w_s, a_b_s, b_f_bias,
              c_mu, c_w0, c_w_up, c_a0, c_a_up, c_g_up, c_k_k, c_k_a, c_r_k,
              c_lnx_g, c_lnx_b, c_v0, c_v_down, c_v_up, p_a, p_b, p_c, w_out,
              norm_ffn, w_gate_up, w_down, norm_final):
    bn, s, _ = x.shape
    v_first = None
    for l in range(DEPTH):
        h = rmsnorm(x, norm_mix[l])
        proj = h @ w_in[l]
        pa, pb, pc, pg = jnp.split(proj, IN_SPLITS, axis=-1)

        ua, va = jnp.split(jax.nn.gelu(pa), 2, axis=-1)
        ya = spatial_gating_unit(ua, va, a_ln_g[l], a_ln_b[l], a_w_s[l], a_b_s[l])

        qb, kb, vb, fb = jnp.split(pb, B_SPLITS, axis=-1)
        hd = (bn, s, B_HEADS, B_HEAD_DIM)
        yb = forgetting_attention(qb.reshape(hd), kb.reshape(hd), vb.reshape(hd), fb, b_f_bias[l])

        pc = pc + (token_shift(pc) - pc) * c_mu[l]
        rc, kc, vc, wlo, alo, glo = jnp.split(pc, C_SPLITS, axis=-1)
        if l == 0:
            v_first = vc
        else:
            vc = vc + (v_first - vc) * jax.nn.sigmoid(c_v0[l - 1] + (vc @ c_v_down[l - 1]) @ c_v_up[l - 1])
        yc = rwkv7_time_mix(rc, kc, vc, wlo, alo, glo, c_w0[l], c_w_up[l], c_a0[l], c_a_up[l],
                            c_g_up[l], c_k_k[l], c_k_a[l], c_r_k[l], c_lnx_g[l], c_lnx_b[l])

        gates = jax.nn.sigmoid(pg.reshape(bn, s, N_BRANCH, D_MODEL) + gate_bias[l])
        merged = (gates[:, :, 0] * (ya @ p_a[l])
                  + gates[:, :, 1] * (yb @ p_b[l])
                  + gates[:, :, 2] * (yc @ p_c[l]))
        x = x + merged @ w_out[l]

        h = rmsnorm(x, norm_ffn[l])
        gt, up = jnp.split(h @ w_gate_up[l], 2, axis=-1)
        x = x + (jax.nn.silu(gt) * up) @ w_down[l]
    return rmsnorm(x, norm_final)
```

```python
import functools

import jax
import jax.numpy as jnp
from jax import lax
from jax.experimental import pallas as pl
from jax.experimental.pallas import tpu as pltpu

F32 = jnp.float32
BF16 = jnp.bfloat16
HIGHEST = lax.Precision.HIGHEST

V7X_VMEM_LIMIT_BYTES = 56 * 1024 * 1024

NORM_EPS = 1e-6
LN_EPS = 1e-5
GN_EPS = 64e-5

N_BRANCH = 3
A_GROUPS = 4
A_CHUNK = 128
B_HEADS = 8
B_HEAD_DIM = 64
C_HEADS = 8
C_HEAD_DIM = 64
C_DECAY_LORA = 64
C_AAA_LORA = 64
C_GATE_LORA = 128
RWKV_CHUNK = 64


def _params(*semantics):
    return pltpu.CompilerParams(dimension_semantics=semantics,
                                vmem_limit_bytes=V7X_VMEM_LIMIT_BYTES)


def _rmsnorm(x, g):
    return x * lax.rsqrt(jnp.mean(x * x, axis=-1, keepdims=True) + NORM_EPS) * g


def _dot(a, b, precision=None):
    return jnp.dot(a, b, preferred_element_type=F32, precision=precision)


def _dot_nt(a, b, precision=None):
    return lax.dot_general(a, b, (((1,), (1,)), ((), ())),
                           preferred_element_type=F32, precision=precision)


def _dot_tn(a, b, precision=None):
    return lax.dot_general(a, b, (((0,), (0,)), ((), ())),
                           preferred_element_type=F32, precision=precision)


def _norm_matmul_kernel(x_ref, g_ref, w_ref, o_ref):
    h = _rmsnorm(x_ref[...], g_ref[...])
    o_ref[...] = _dot(h.astype(BF16), w_ref[...])


def norm_matmul(x, g, w, tm):
    n, d = x.shape
    cols = w.shape[1]
    return pl.pallas_call(
        _norm_matmul_kernel,
        grid=(n // tm,),
        in_specs=[pl.BlockSpec((tm, d), lambda i: (i, 0)),
                  pl.BlockSpec((1, d), lambda i: (0, 0)),
                  pl.BlockSpec((d, cols), lambda i: (0, 0))],
        out_specs=pl.BlockSpec((tm, cols), lambda i: (i, 0)),
        out_shape=jax.ShapeDtypeStruct((n, cols), F32),
        compiler_params=_params("parallel"),
        name="norm_matmul",
    )(x, g, w)


def _gmlp_kernel(pa_ref, lng_ref, lnb_ref, ws_ref, bs_ref, o_ref, *, width):
    tm = pa_ref.shape[0]
    gd = width // A_GROUPS
    row = lax.broadcasted_iota(jnp.int32, (A_CHUNK, A_CHUNK), 0)
    col = lax.broadcasted_iota(jnp.int32, (A_CHUNK, A_CHUNK), 1)
    causal = col <= row
    for c in range(tm // A_CHUNK):
        rows = slice(c * A_CHUNK, (c + 1) * A_CHUNK)
        act = jax.nn.gelu(pa_ref[rows, :])
        u = act[:, :width]
        v = act[:, width:]
        mu = jnp.mean(v, axis=-1, keepdims=True)
        var = jnp.mean(jnp.square(v - mu), axis=-1, keepdims=True)
        v = (v - mu) * lax.rsqrt(var + LN_EPS) * lng_ref[...] + lnb_ref[...]
        v = v.astype(BF16)
        for g in range(A_GROUPS):
            cols = slice(g * gd, (g + 1) * gd)
            w = jnp.where(causal, ws_ref[g], 0.0).astype(BF16)
            mixed = _dot(w, v[:, cols]) + bs_ref[:, cols]
            o_ref[rows, cols] = u[:, cols] * mixed


def gmlp(pa, ln_g, ln_b, w_s, b_s_full, tm):
    n, two_w = pa.shape
    width = two_w // 2
    return pl.pallas_call(
        functools.partial(_gmlp_kernel, width=width),
        grid=(n // tm,),
        in_specs=[pl.BlockSpec((tm, two_w), lambda i: (i, 0)),
                  pl.BlockSpec((1, width), lambda i: (0, 0)),
                  pl.BlockSpec((1, width), lambda i: (0, 0)),
                  pl.BlockSpec((A_GROUPS, A_CHUNK, A_CHUNK), lambda i: (0, 0, 0)),
                  pl.BlockSpec((A_CHUNK, width), lambda i: (0, 0))],
        out_specs=pl.BlockSpec((tm, width), lambda i: (i, 0)),
        out_shape=jax.ShapeDtypeStruct((n, width), F32),
        compiler_params=_params("parallel"),
        name="gmlp",
    )(pa, ln_g, ln_b, w_s, b_s_full)


def _logf_cumsum_kernel(f_ref, b_ref, o_ref):
    rows = f_ref.shape[0]
    log_f = jax.nn.log_sigmoid(f_ref[...] + b_ref[...])
    r = lax.broadcasted_iota(jnp.int32, (128, 128), 0)
    c = lax.broadcasted_iota(jnp.int32, (128, 128), 1)
    within = _dot(log_f, (r <= c).astype(F32), HIGHEST)
    rr = lax.broadcasted_iota(jnp.int32, (rows, rows), 0)
    rc = lax.broadcasted_iota(jnp.int32, (rows, rows), 1)
    before = _dot((rc < rr).astype(F32), log_f, HIGHEST)
    o_ref[...] = within + jnp.sum(before, axis=-1, keepdims=True)


def logf_cumsum(f, bias):
    bh, rows, _ = f.shape
    return pl.pallas_call(
        _logf_cumsum_kernel,
        grid=(bh,),
        in_specs=[pl.BlockSpec((None, rows, 128), lambda i: (i, 0, 0)),
                  pl.BlockSpec((None, 1, 1), lambda i: (i, 0, 0))],
        out_specs=pl.BlockSpec((None, rows, 128), lambda i: (i, 0, 0)),
        out_shape=jax.ShapeDtypeStruct(f.shape, F32),
        compiler_params=_params("parallel"),
        name="logf_cumsum",
    )(f, bias)


def _fox_kernel(q_ref, k_ref, v_ref, cq_ref, ck_ref, o_ref, m_scr, l_scr, acc_scr, *, scale):
    i = pl.program_id(2)
    j = pl.program_id(3)
    tq, tk = q_ref.shape[0], k_ref.shape[0]

    @pl.when(j == 0)
    def _():
        m_scr[...] = jnp.full(m_scr.shape, -jnp.inf, F32)
        l_scr[...] = jnp.zeros(l_scr.shape, F32)
        acc_scr[...] = jnp.zeros(acc_scr.shape, F32)

    def update(masked):
        q = (q_ref[...] * scale).astype(BF16)
        s = _dot_nt(q, k_ref[...].astype(BF16))
        s = s + (cq_ref[...] - ck_ref[...])
        if masked:
            row = lax.broadcasted_iota(jnp.int32, (tq, tk), 0)
            col = lax.broadcasted_iota(jnp.int32, (tq, tk), 1)
            s = jnp.where(col <= row, s, -jnp.inf)
        m_prev = m_scr[...]
        m_new = jnp.maximum(m_prev, jnp.max(s, axis=-1, keepdims=True))
        alpha = jnp.exp(m_prev - m_new)
        p = jnp.exp(s - m_new)
        l_scr[...] = alpha * l_scr[...] + jnp.sum(p, axis=-1, keepdims=True)
        acc_scr[...] = alpha * acc_scr[...] + _dot(p.astype(BF16), v_ref[...].astype(BF16))
        m_scr[...] = m_new

    @pl.when(j < i)
    def _():
        update(False)

    @pl.when(j == i)
    def _():
        update(True)
        o_ref[...] = acc_scr[...] / l_scr[...]


def fox_attention(q, k, v, c_col, c_row, tq):
    b, h, s, d = q.shape
    nq = s // tq
    kv_idx = lambda bi, hi, i, j: (bi, hi, jnp.minimum(i, j), 0)
    return pl.pallas_call(
        functools.partial(_fox_kernel, scale=d ** -0.5),
        grid=(b, h, nq, nq),
        in_specs=[pl.BlockSpec((None, None, tq, d), lambda bi, hi, i, j: (bi, hi, i, 0)),
                  pl.BlockSpec((None, None, tq, d), kv_idx),
                  pl.BlockSpec((None, None, tq, d), kv_idx),
                  pl.BlockSpec((None, None, tq, 1), lambda bi, hi, i, j: (bi, hi, i, 0)),
                  pl.BlockSpec((None, None, 1, tq), lambda bi, hi, i, j: (bi, hi, 0, jnp.minimum(i, j)))],
        out_specs=pl.BlockSpec((None, None, tq, d), lambda bi, hi, i, j: (bi, hi, i, 0)),
        out_shape=jax.ShapeDtypeStruct(q.shape, F32),
        scratch_shapes=[pltpu.VMEM((tq, 1), F32), pltpu.VMEM((tq, 1), F32),
                        pltpu.VMEM((tq, d), F32)],
        compiler_params=_params("parallel", "parallel", "parallel", "arbitrary"),
        name="fox_attention",
    )(q, k, v, c_col, c_row)


def _rwkv_prep_kernel(*refs, seq, mix_value):
    if mix_value:
        (pc_ref, prev_ref, mu_ref, w0_ref, wup_ref, a0_ref, aup_ref, gup_ref, kk_ref, ka_ref,
         hsum_ref, vfirst_ref, v0_ref, vdown_ref, vup_ref,
         r_out, ld_out, k_out, v_out, kk_out, b_out, g_out) = refs
    else:
        (pc_ref, prev_ref, mu_ref, w0_ref, wup_ref, a0_ref, aup_ref, gup_ref, kk_ref, ka_ref,
         hsum_ref,
         r_out, ld_out, k_out, v_out, kk_out, b_out, g_out) = refs
    tm = pc_ref.shape[0]
    cw = r_out.shape[1]
    pc = pc_ref[...]
    at_seq_start = (pl.program_id(0) * tm) % seq == 0
    prev_row = jnp.where(at_seq_start, 0.0, prev_ref[7:8, :])
    row = lax.broadcasted_iota(jnp.int32, pc.shape, 0)
    shifted = jnp.where(row == 0, prev_row, pltpu.roll(pc, 1, axis=0))
    pc = pc + (shifted - pc) * mu_ref[...]

    r = pc[:, 0:cw]
    k = pc[:, cw:2 * cw]
    v = pc[:, 2 * cw:3 * cw]
    lora = pc[:, 3 * cw:3 * cw + C_DECAY_LORA + C_AAA_LORA]
    g_lo = pc[:, 3 * cw + C_DECAY_LORA + C_AAA_LORA:]
    if mix_value:
        lam = jax.nn.sigmoid(v0_ref[...] + _dot(_dot(v, vdown_ref[...], HIGHEST), vup_ref[...], HIGHEST))
        v = v + (vfirst_ref[...] - v) * lam
    w = -jax.nn.softplus(-(w0_ref[...] + _dot(jnp.tanh(lora), wup_ref[...], HIGHEST))) - 0.5
    log_decay = -jnp.exp(w)
    a = jax.nn.sigmoid(a0_ref[...] + _dot(lora, aup_ref[...], HIGHEST))
    g = _dot(jax.nn.sigmoid(g_lo), gup_ref[...], HIGHEST)
    kk = k * kk_ref[...]
    ss = _dot(kk * kk, hsum_ref[...], HIGHEST)
    kk = kk / jnp.maximum(jnp.sqrt(ss), 1e-12)
    k = k * (1.0 + (a - 1.0) * ka_ref[...])
    r_out[...] = r
    ld_out[...] = log_decay
    k_out[...] = k
    v_out[...] = v
    kk_out[...] = kk
    b_out[...] = kk * a
    g_out[...] = g


def rwkv_prep(pc, mu, w0, w_up_pad, a0, a_up_pad, g_up, k_k, k_a, head_sum, seq, tm,
              v_first=None, v0=None, v_down=None, v_up=None):
    n, ccols = pc.shape
    cw = w0.shape[1]
    mix_value = v_first is not None
    const = lambda shape: pl.BlockSpec(shape, lambda i: (0,) * len(shape))
    in_specs = [pl.BlockSpec((tm, ccols), lambda i: (i, 0)),
                pl.BlockSpec((8, ccols), lambda i: (jnp.maximum(i * (tm // 8) - 1, 0), 0)),
                const((1, ccols)), const((1, cw)), const(w_up_pad.shape), const((1, cw)),
                const(a_up_pad.shape), const(g_up.shape), const((1, cw)), const((1, cw)),
                const(head_sum.shape)]
    args = [pc, pc, mu, w0, w_up_pad, a0, a_up_pad, g_up, k_k, k_a, head_sum]
    if mix_value:
        in_specs += [pl.BlockSpec((tm, cw), lambda i: (i, 0)), const((1, cw)),
                     const(v_down.shape), const(v_up.shape)]
        args += [v_first, v0, v_down, v_up]
    out = jax.ShapeDtypeStruct((n, cw), F32)
    return pl.pallas_call(
        functools.partial(_rwkv_prep_kernel, seq=seq, mix_value=mix_value),
        grid=(n // tm,),
        in_specs=in_specs,
        out_specs=[pl.BlockSpec((tm, cw), lambda i: (i, 0))] * 7,
        out_shape=[out] * 7,
        compiler_params=_params("parallel"),
        name="rwkv_prep",
    )(*args)


def _unit_lower_inverse(a_strict, n):
    row = lax.broadcasted_iota(jnp.int32, (n, n), 0)
    col = lax.broadcasted_iota(jnp.int32, (n, n), 1)
    x = (row == col).astype(F32)
    m = 1
    while m < n:
        lower_left = ((row // (2 * m)) == (col // (2 * m))) & ((row // m) % 2 == 1) & ((col // m) % 2 == 0)
        l21 = jnp.where(lower_left, a_strict, 0.0)
        x = x - _dot(_dot(x, l21, HIGHEST), x, HIGHEST)
        m *= 2
    return x


def _rwkv_scan_kernel(r_ref, ld_ref, k_ref, v_ref, kk_ref, b_ref, rk_ref, lng_ref, lnb_ref,
                      o_ref, state_scr):
    tc = r_ref.shape[0]
    c = RWKV_CHUNK
    n = r_ref.shape[1]

    @pl.when(pl.program_id(2) == 0)
    def _():
        state_scr[...] = jnp.zeros(state_scr.shape, F32)

    row = lax.broadcasted_iota(jnp.int32, (c, c), 0)
    col = lax.broadcasted_iota(jnp.int32, (c, c), 1)
    incl = col <= row
    strict = col < row
    tril_ones = incl.astype(F32)
    eye_n = (lax.broadcasted_iota(jnp.int32, (n, n), 0)
             == lax.broadcasted_iota(jnp.int32, (n, n), 1)).astype(F32)

    for ci in range(tc // c):
        rows = slice(ci * c, (ci + 1) * c)
        r = r_ref[rows, :]
        ld = ld_ref[rows, :]
        k = k_ref[rows, :]
        v = v_ref[rows, :]
        kk = kk_ref[rows, :]
        b = b_ref[rows, :]

        cum = _dot(tril_ones, ld, HIGHEST)
        p_incl = jnp.exp(cum)
        p_inv = jnp.exp(-cum)
        p_prev = jnp.exp(cum - ld)
        r_t = r * p_incl
        kk_t = kk * p_prev
        b_t = b * p_inv
        k_t = k * p_inv

        pair = _dot_nt(jnp.concatenate([kk_t, r_t], axis=0),
                       jnp.concatenate([b_t, k_t], axis=0), HIGHEST)
        a_b = jnp.where(strict, pair[:c, :c], 0.0)
        a_k = jnp.where(strict, pair[:c, c:], 0.0)
        a_rb = jnp.where(incl, pair[c:, :c], 0.0)
        a_rk = jnp.where(incl, pair[c:, c:], 0.0)

        t_inv = _unit_lower_inverse(a_b, c)
        w = _dot(t_inv, kk_t, HIGHEST)
        u0 = _dot(t_inv, _dot(a_k, v, HIGHEST), HIGHEST)
        r_q = r_t - _dot(a_rb, w, HIGHEST)
        y0 = _dot(a_rk, v, HIGHEST) - _dot(a_rb, u0, HIGHEST)
        p_end = p_incl[c - 1:c, :]
        g_mat = (eye_n - _dot_tn(w, b_t, HIGHEST)) * p_end
        h_mat = (_dot_tn(v, k_t, HIGHEST) - _dot_tn(u0, b_t, HIGHEST)) * p_end

        s0 = state_scr[...]
        y = _dot_nt(r_q, s0, HIGHEST) + y0
        state_scr[...] = _dot(s0, g_mat, HIGHEST) + h_mat

        mu = jnp.mean(y, axis=-1, keepdims=True)
        var = jnp.mean(jnp.square(y - mu), axis=-1, keepdims=True)
        y = (y - mu) * lax.rsqrt(var + GN_EPS) * lng_ref[...] + lnb_ref[...]
        bonus = jnp.sum(r * k * rk_ref[...], axis=-1, keepdims=True) * v
        o_ref[rows, :] = y + bonus


def rwkv_scan(r, ld, k, v, kk, b, r_k, lnx_g, lnx_b, tc):
    bn, h, s, n = r.shape
    seq_spec = pl.BlockSpec((None, None, tc, n), lambda bi, hi, t: (bi, hi, t, 0))
    head_spec = pl.BlockSpec((None, 1, n), lambda bi, hi, t: (hi, 0, 0))
    return pl.pallas_call(
        _rwkv_scan_kernel,
        grid=(bn, h, s // tc),
        in_specs=[seq_spec] * 6 + [head_spec] * 3,
        out_specs=seq_spec,
        out_shape=jax.ShapeDtypeStruct(r.shape, F32),
        scratch_shapes=[pltpu.VMEM((n, n), F32)],
        compiler_params=_params("parallel", "parallel", "arbitrary"),
        name="rwkv_scan",
    )(r, ld, k, v, kk, b, r_k, lnx_g, lnx_b)


def _merge_kernel(x_ref, ya_ref, yb_ref, yc_ref, g_ref, pg_ref, gb_ref,
                  pa_ref, pb_ref, pc_ref, wo_ref, o_ref):
    d = x_ref.shape[1]
    branches = (_dot(ya_ref[...].astype(BF16), pa_ref[...]),
                _dot(yb_ref[...].astype(BF16), pb_ref[...]),
                _dot((yc_ref[...] * g_ref[...]).astype(BF16), pc_ref[...]))
    merged = jnp.zeros_like(branches[0])
    for bi in range(N_BRANCH):
        gate = jax.nn.sigmoid(pg_ref[:, bi * d:(bi + 1) * d] + gb_ref[bi:bi + 1, :])
        merged = merged + gate * branches[bi]
    o_ref[...] = x_ref[...] + _dot(merged.astype(BF16), wo_ref[...])


def merge_out(x, ya, yb, yc, g, pg, gate_bias, p_a, p_b, p_c, w_out, tm):
    n, d = x.shape
    tile = lambda a: pl.BlockSpec((tm, a.shape[1]), lambda i: (i, 0))
    const = lambda a: pl.BlockSpec(a.shape, lambda i: (0, 0))
    return pl.pallas_call(
        _merge_kernel,
        grid=(n // tm,),
        in_specs=[tile(x), tile(ya), tile(yb), tile(yc), tile(g), tile(pg), const(gate_bias),
                  const(p_a), const(p_b), const(p_c), const(w_out)],
        out_specs=pl.BlockSpec((tm, d), lambda i: (i, 0)),
        out_shape=jax.ShapeDtypeStruct((n, d), F32),
        compiler_params=_params("parallel"),
        name="merge_out",
    )(x, ya, yb, yc, g, pg, gate_bias, p_a, p_b, p_c, w_out)


def _ffn_kernel(x_ref, g_ref, wg_ref, wu_ref, wd_ref, gf_ref, o_ref, h_scr, acc_scr, *, final_norm):
    j = pl.program_id(1)

    @pl.when(j == 0)
    def _():
        x = x_ref[...]
        h_scr[...] = _rmsnorm(x, g_ref[...]).astype(BF16)
        acc_scr[...] = x

    h = h_scr[...]
    act = jax.nn.silu(_dot(h, wg_ref[...])) * _dot(h, wu_ref[...])
    acc_scr[...] += _dot(act.astype(BF16), wd_ref[...])

    @pl.when(j == pl.num_programs(1) - 1)
    def _():
        y = acc_scr[...]
        o_ref[...] = _rmsnorm(y, gf_ref[...]) if final_norm else y


def ffn(x, g, w_gate_up, w_down, g_final, tm, tf, final_norm):
    n, d = x.shape
    dff = w_down.shape[0]
    nf = dff // tf
    return pl.pallas_call(
        functools.partial(_ffn_kernel, final_norm=final_norm),
        grid=(n // tm, nf),
        in_specs=[pl.BlockSpec((tm, d), lambda i, j: (i, 0)),
                  pl.BlockSpec((1, d), lambda i, j: (0, 0)),
                  pl.BlockSpec((d, tf), lambda i, j: (0, j)),
                  pl.BlockSpec((d, tf), lambda i, j: (0, j + nf)),
                  pl.BlockSpec((tf, d), lambda i, j: (j, 0)),
                  pl.BlockSpec((1, d), lambda i, j: (0, 0))],
        out_specs=pl.BlockSpec((tm, d), lambda i, j: (i, 0)),
        out_shape=jax.ShapeDtypeStruct((n, d), F32),
        scratch_shapes=[pltpu.VMEM((tm, d), BF16), pltpu.VMEM((tm, d), F32)],
        compiler_params=_params("parallel", "arbitrary"),
        name="ffn",
    )(x, g, w_gate_up, w_gate_up, w_down, g_final)


def _pad_cols(w, mult):
    pad = (-w.shape[1]) % mult
    return jnp.pad(w, ((0, 0), (0, pad))) if pad else w


def _largest_tile(total, cap):
    t = cap
    while total % t:
        t //= 2
    return t


def kernel(x, norm_mix, w_in, gate_bias, a_ln_g, a_ln_b, a_w_s, a_b_s, b_f_bias, c_mu, c_w0, c_w_up, c_a0, c_a_up, c_g_up, c_k_k, c_k_a, c_r_k, c_lnx_g, c_lnx_b, c_v0, c_v_down, c_v_up, p_a, p_b, p_c, w_out, norm_ffn, w_gate_up, w_down, norm_final):
    bn, s, d = x.shape
    depth = w_in.shape[0]
    n = bn * s
    a_width = a_ln_g.shape[1]
    b_width = B_HEADS * B_HEAD_DIM
    c_width = C_HEADS * C_HEAD_DIM
    a_cols = 2 * a_width
    b_cols = 3 * b_width + B_HEADS
    c_cols = 3 * c_width + C_DECAY_LORA + C_AAA_LORA + C_GATE_LORA
    dff = w_down.shape[1]
    assert s % 128 == 0 and dff % 256 == 0

    tm = _largest_tile(n, 512)
    t_attn = _largest_tile(s, 512)
    t_scan = _largest_tile(s, 512)
    tf = dff // 2 if (dff // 2) % 128 == 0 else dff

    head_sum = jnp.kron(jnp.eye(C_HEADS, dtype=F32), jnp.ones((C_HEAD_DIM, C_HEAD_DIM), F32))
    row2 = lambda p: p.reshape(1, -1)
    to_heads = lambda t, h: t.reshape(bn, s, h, -1).transpose(0, 2, 1, 3)
    from_heads = lambda t: t.transpose(0, 2, 1, 3).reshape(n, -1)

    xf = x.reshape(n, d)
    v_first = None
    for l in range(depth):
        w_l = w_in[l].astype(BF16)
        w_a = w_l[:, :a_cols]
        w_b = _pad_cols(w_l[:, a_cols:a_cols + b_cols], 128)
        w_c = w_l[:, a_cols + b_cols:a_cols + b_cols + c_cols]
        w_g = w_l[:, a_cols + b_cols + c_cols:]
        g_mix = row2(norm_mix[l])
        pa = norm_matmul(xf, g_mix, w_a, tm)
        pb = norm_matmul(xf, g_mix, w_b, tm)
        pc = norm_matmul(xf, g_mix, w_c, tm)
        pg = norm_matmul(xf, g_mix, w_g, tm)

        b_s_full = jnp.repeat(a_b_s[l].T, a_width // A_GROUPS, axis=1)
        ya = gmlp(pa, row2(a_ln_g[l]), row2(a_ln_b[l]), a_w_s[l], b_s_full, _largest_tile(n, 256))

        q = to_heads(pb[:, :b_width], B_HEADS)
        k = to_heads(pb[:, b_width:2 * b_width], B_HEADS)
        v = to_heads(pb[:, 2 * b_width:3 * b_width], B_HEADS)
        f = pb[:, 3 * b_width:3 * b_width + B_HEADS].reshape(bn, s, B_HEADS).transpose(0, 2, 1)
        f_bias = jnp.tile(b_f_bias[l], bn).reshape(bn * B_HEADS, 1, 1)
        c = logf_cumsum(f.reshape(bn * B_HEADS, s // 128, 128), f_bias)
        c = c.reshape(bn, B_HEADS, s)
        yb = fox_attention(q, k, v, c[..., None], c[:, :, None, :], t_attn)
        yb = from_heads(yb)

        zeros_lora = jnp.zeros((C_AAA_LORA, c_width), F32)
        w_up_pad = jnp.concatenate([c_w_up[l], zeros_lora], axis=0)
        a_up_pad = jnp.concatenate([jnp.zeros((C_DECAY_LORA, c_width), F32), c_a_up[l]], axis=0)
        mix = {} if l == 0 else dict(v_first=v_first, v0=row2(c_v0[l - 1]),
                                     v_down=c_v_down[l - 1], v_up=c_v_up[l - 1])
        r_c, ld_c, k_c, v_c, kk_c, b_c, g_c = rwkv_prep(
            pc, row2(c_mu[l]), row2(c_w0[l]), w_up_pad, row2(c_a0[l]), a_up_pad, c_g_up[l],
            row2(c_k_k[l]), row2(c_k_a[l]), head_sum, s, tm, **mix)
        if l == 0:
            v_first = v_c
        heads_c = lambda t: to_heads(t, C_HEADS)
        per_head = lambda p: p.reshape(C_HEADS, 1, C_HEAD_DIM)
        yc = rwkv_scan(heads_c(r_c), heads_c(ld_c), heads_c(k_c), heads_c(v_c), heads_c(kk_c),
                       heads_c(b_c), per_head(c_r_k[l]), per_head(c_lnx_g[l]), per_head(c_lnx_b[l]),
                       t_scan)
        yc = from_heads(yc)

        xf = merge_out(xf, ya, yb, yc, g_c, pg, gate_bias[l], p_a[l].astype(BF16),
                       p_b[l].astype(BF16), p_c[l].astype(BF16), w_out[l].astype(BF16), tm)
        xf = ffn(xf, row2(norm_ffn[l]), w_gate_up[l].astype(BF16), w_down[l].astype(BF16),
                 row2(norm_final), tm, tf, final_norm=(l == depth - 1))
    return xf.reshape(bn, s, d)
```

```python
import functools

import jax
import jax.numpy as jnp
from jax import lax
from jax.experimental import pallas as pl
from jax.experimental.pallas import tpu as pltpu

F32 = jnp.float32
BF16 = jnp.bfloat16
HIGHEST = lax.Precision.HIGHEST

V7X_VMEM_LIMIT_BYTES = 56 * 1024 * 1024

NORM_EPS = 1e-6
LN_EPS = 1e-5
GN_EPS = 64e-5

N_BRANCH = 3
A_GROUPS = 4
A_CHUNK = 128
B_HEADS = 8
B_HEAD_DIM = 64
C_HEADS = 8
C_HEAD_DIM = 64
C_DECAY_LORA = 64
C_AAA_LORA = 64
C_GATE_LORA = 128
LOG2_E = 1.4426950408889634
FOX_V_ROWS = 80
RWKV_CHUNK = 64
RWKV_PACK = 2


def _params(*semantics):
    return pltpu.CompilerParams(dimension_semantics=semantics,
                                vmem_limit_bytes=V7X_VMEM_LIMIT_BYTES)


def _rmsnorm(x, g):
    return x * lax.rsqrt(jnp.mean(x * x, axis=-1, keepdims=True) + NORM_EPS) * g


def _dot(a, b, precision=None):
    return jnp.dot(a, b, preferred_element_type=F32, precision=precision)


def _dot_nt(a, b, precision=None):
    return lax.dot_general(a, b, (((1,), (1,)), ((), ())),
                           preferred_element_type=F32, precision=precision)


def _dot_tn(a, b, precision=None):
    return lax.dot_general(a, b, (((0,), (0,)), ((), ())),
                           preferred_element_type=F32, precision=precision)


def _norm_matmul_kernel(x_ref, g_ref, w_ref, o_ref):
    h = _rmsnorm(x_ref[...], g_ref[...])
    o_ref[...] = _dot(h.astype(BF16), w_ref[...])


def norm_matmul(x, g, w, tm):
    n, d = x.shape
    cols = w.shape[1]
    return pl.pallas_call(
        _norm_matmul_kernel,
        grid=(n // tm,),
        in_specs=[pl.BlockSpec((tm, d), lambda i: (i, 0)),
                  pl.BlockSpec((1, d), lambda i: (0, 0)),
                  pl.BlockSpec((d, cols), lambda i: (0, 0))],
        out_specs=pl.BlockSpec((tm, cols), lambda i: (i, 0)),
        out_shape=jax.ShapeDtypeStruct((n, cols), F32),
        compiler_params=_params("parallel"),
        name="norm_matmul",
    )(x, g, w)


def _gmlp_kernel(pa_ref, lng_ref, lnb_ref, ws_ref, bs_ref, o_ref, *, width):
    tm = pa_ref.shape[0]
    gd = width // A_GROUPS
    row = lax.broadcasted_iota(jnp.int32, (A_CHUNK, A_CHUNK), 0)
    col = lax.broadcasted_iota(jnp.int32, (A_CHUNK, A_CHUNK), 1)
    causal = col <= row
    for c in range(tm // A_CHUNK):
        rows = slice(c * A_CHUNK, (c + 1) * A_CHUNK)
        act = jax.nn.gelu(pa_ref[rows, :])
        u = act[:, :width]
        v = act[:, width:]
        mu = jnp.mean(v, axis=-1, keepdims=True)
        var = jnp.mean(jnp.square(v - mu), axis=-1, keepdims=True)
        v = (v - mu) * lax.rsqrt(var + LN_EPS) * lng_ref[...] + lnb_ref[...]
        v = v.astype(BF16)
        for g in range(A_GROUPS):
            cols = slice(g * gd, (g + 1) * gd)
            w = jnp.where(causal, ws_ref[g], 0.0).astype(BF16)
            mixed = _dot(w, v[:, cols]) + bs_ref[:, cols]
            o_ref[rows, cols] = u[:, cols] * mixed


def gmlp(pa, ln_g, ln_b, w_s, b_s_full, tm):
    n, two_w = pa.shape
    width = two_w // 2
    return pl.pallas_call(
        functools.partial(_gmlp_kernel, width=width),
        grid=(n // tm,),
        in_specs=[pl.BlockSpec((tm, two_w), lambda i: (i, 0)),
                  pl.BlockSpec((1, width), lambda i: (0, 0)),
                  pl.BlockSpec((1, width), lambda i: (0, 0)),
                  pl.BlockSpec((A_GROUPS, A_CHUNK, A_CHUNK), lambda i: (0, 0, 0)),
                  pl.BlockSpec((A_CHUNK, width), lambda i: (0, 0))],
        out_specs=pl.BlockSpec((tm, width), lambda i: (i, 0)),
        out_shape=jax.ShapeDtypeStruct((n, width), F32),
        compiler_params=_params("parallel"),
        name="gmlp",
    )(pa, ln_g, ln_b, w_s, b_s_full)


def _one_hot(shape, hit):
    r = lax.broadcasted_iota(jnp.int32, shape, 0)
    c = lax.broadcasted_iota(jnp.int32, shape, 1)
    return jnp.where(hit(r, c), 1.0, 0.0).astype(BF16)


def _fox_prep_kernel(pb_ref, fb_ref, qa_ref, ka_ref, vt_ref, carry_scr, *, seq, scale):
    tm = pb_ref.shape[0]
    hd = B_HEAD_DIM
    width = B_HEADS * hd
    x = pb_ref[...]

    @pl.when((pl.program_id(0) * tm) % seq == 0)
    def _():
        carry_scr[...] = jnp.zeros(carry_scr.shape, F32)

    log_f = jax.nn.log_sigmoid(x[:, 3 * width:3 * width + 128] + fb_ref[...]) * LOG2_E
    tril = (lax.broadcasted_iota(jnp.int32, (tm, tm), 1)
            <= lax.broadcasted_iota(jnp.int32, (tm, tm), 0)).astype(F32)
    cum = _dot(tril, log_f, HIGHEST) + carry_scr[...]
    carry_scr[...] = cum[tm - 1:tm, :]
    neg = -cum
    c_hi = neg.astype(BF16)
    rest = neg - c_hi.astype(F32)
    c_mid = rest.astype(BF16)
    c_lo = (rest - c_mid.astype(F32)).astype(BF16)
    c_split = jnp.concatenate([c_hi, c_mid, c_lo], axis=1)

    lane = lax.broadcasted_iota(jnp.int32, (tm, 128), 1)
    ones_lanes = (lane >= hd) & (lane < hd + 3)
    row_t = lax.broadcasted_iota(jnp.int32, (vt_ref.shape[1], tm), 0)
    for h in range(B_HEADS):
        lanes = slice((h // 2) * 128, (h // 2) * 128 + 128)
        off = (h % 2) * hd
        to_low = _one_hot((128, 128), lambda r, c: (c < hd) & (r == c + off))
        place_c = _one_hot((384, 128), lambda r, c: (r % 128 == h) & (c == hd + r // 128))
        v_rows = _one_hot((vt_ref.shape[1], 128), lambda r, c: (r < hd) & (c == r + off))
        q = (x[:, lanes] * (scale * LOG2_E)).astype(BF16)
        k = x[:, width:2 * width][:, lanes].astype(BF16)
        v = x[:, 2 * width:3 * width][:, lanes].astype(BF16)
        qa_ref[h] = jnp.where(ones_lanes, 1.0, _dot(q, to_low)).astype(BF16)
        ka_ref[h] = (_dot(k, to_low) + _dot(c_split, place_c)).astype(BF16)
        vt_ref[h] = jnp.where(row_t == hd, 1.0, _dot_nt(v_rows, v)).astype(BF16)


def fox_prep(pb, f_bias_row, seq, tm):
    n = pb.shape[0]
    return pl.pallas_call(
        functools.partial(_fox_prep_kernel, seq=seq, scale=B_HEAD_DIM ** -0.5),
        grid=(n // tm,),
        in_specs=[pl.BlockSpec((tm, pb.shape[1]), lambda i: (i, 0)),
                  pl.BlockSpec((1, 128), lambda i: (0, 0))],
        out_specs=[pl.BlockSpec((B_HEADS, tm, 128), lambda i: (0, i, 0)),
                   pl.BlockSpec((B_HEADS, tm, 128), lambda i: (0, i, 0)),
                   pl.BlockSpec((B_HEADS, FOX_V_ROWS, tm), lambda i: (0, 0, i))],
        out_shape=[jax.ShapeDtypeStruct((B_HEADS, n, 128), BF16),
                   jax.ShapeDtypeStruct((B_HEADS, n, 128), BF16),
                   jax.ShapeDtypeStruct((B_HEADS, FOX_V_ROWS, n), BF16)],
        scratch_shapes=[pltpu.VMEM((1, 128), F32)],
        compiler_params=_params("arbitrary"),
        name="fox_prep",
    )(pb, f_bias_row)


def _fox_kernel(qa_ref, ka_ref, vt_ref, o_ref, m_scr, acc_scr):
    i = pl.program_id(2)
    j = pl.program_id(3)
    tq, tk = qa_ref.shape[1], ka_ref.shape[1]
    hd = B_HEAD_DIM

    @pl.when(j == 0)
    def _():
        m_scr[...] = jnp.full(m_scr.shape, -jnp.inf, F32)
        acc_scr[...] = jnp.zeros(acc_scr.shape, F32)

    def update(masked):
        if masked:
            key = lax.broadcasted_iota(jnp.int32, (tk, tq), 0)
            qry = lax.broadcasted_iota(jnp.int32, (tk, tq), 1)
            causal = key <= qry
        for h in range(2):
            s = _dot_nt(ka_ref[h], qa_ref[h])
            if masked:
                s = jnp.where(causal, s, -jnp.inf)
            m_prev = m_scr[h]
            m_new = jnp.maximum(m_prev, jnp.max(s, axis=0, keepdims=True))
            alpha = jnp.exp2(m_prev - m_new)
            p = jnp.exp2((s - m_new).astype(BF16))
            acc_scr[h] = alpha * acc_scr[h] + _dot(vt_ref[h], p)
            m_scr[h] = m_new

    @pl.when(j < i)
    def _():
        update(False)

    @pl.when(j == i)
    def _():
        update(True)
        out_t = [acc_scr[h, 0:hd, :] / acc_scr[h, hd:hd + 1, :] for h in range(2)]
        o_ref[...] = jnp.concatenate(out_t, axis=0).T


def fox_attention(qa, ka, vt, bn, seq, tq):
    heads, n, _ = qa.shape
    rows = vt.shape[1]
    nq = seq // tq
    kv_blk = lambda bi, i, j: bi * nq + jnp.minimum(i, j)
    return pl.pallas_call(
        _fox_kernel,
        grid=(bn, heads // 2, nq, nq),
        in_specs=[pl.BlockSpec((2, tq, 128), lambda bi, hi, i, j: (hi, bi * nq + i, 0)),
                  pl.BlockSpec((2, tq, 128), lambda bi, hi, i, j: (hi, kv_blk(bi, i, j), 0)),
                  pl.BlockSpec((2, rows, tq), lambda bi, hi, i, j: (hi, 0, kv_blk(bi, i, j)))],
        out_specs=pl.BlockSpec((tq, 128), lambda bi, hi, i, j: (bi * nq + i, hi)),
        out_shape=jax.ShapeDtypeStruct((n, heads * B_HEAD_DIM), F32),
        scratch_shapes=[pltpu.VMEM((2, 1, tq), F32), pltpu.VMEM((2, rows, tq), F32)],
        compiler_params=_params("parallel", "parallel", "parallel", "arbitrary"),
        name="fox_attention",
    )(qa, ka, vt)


def _rwkv_prep_kernel(*refs, seq, mix_value):
    if mix_value:
        (pc_ref, prev_ref, mu_ref, w0_ref, wup_ref, a0_ref, aup_ref, gup_ref, kk_ref, ka_ref,
         rk_ref, hsum_ref, vfirst_ref, v0_ref, vdown_ref, vup_ref,
         r_out, ld_out, k_out, v_out, kk_out, b_out, g_out, bonus_out) = refs
    else:
        (pc_ref, prev_ref, mu_ref, w0_ref, wup_ref, a0_ref, aup_ref, gup_ref, kk_ref, ka_ref,
         rk_ref, hsum_ref,
         r_out, ld_out, k_out, v_out, kk_out, b_out, g_out, bonus_out) = refs
    tm = pc_ref.shape[0]
    cw = r_out.shape[1]
    pc = pc_ref[...]
    at_seq_start = (pl.program_id(0) * tm) % seq == 0
    prev_row = jnp.where(at_seq_start, 0.0, prev_ref[7:8, :])
    row = lax.broadcasted_iota(jnp.int32, pc.shape, 0)
    shifted = jnp.where(row == 0, prev_row, pltpu.roll(pc, 1, axis=0))
    pc = pc + (shifted - pc) * mu_ref[...]

    r = pc[:, 0:cw]
    k = pc[:, cw:2 * cw]
    v = pc[:, 2 * cw:3 * cw]
    lora = pc[:, 3 * cw:3 * cw + C_DECAY_LORA + C_AAA_LORA]
    g_lo = pc[:, 3 * cw + C_DECAY_LORA + C_AAA_LORA:]
    if mix_value:
        lam = jax.nn.sigmoid(v0_ref[...] + _dot(_dot(v, vdown_ref[...], HIGHEST), vup_ref[...], HIGHEST))
        v = v + (vfirst_ref[...] - v) * lam
    w = -jax.nn.softplus(-(w0_ref[...] + _dot(jnp.tanh(lora), wup_ref[...], HIGHEST))) - 0.5
    log_decay = -jnp.exp(w)
    a = jax.nn.sigmoid(a0_ref[...] + _dot(lora, aup_ref[...], HIGHEST))
    g = _dot(jax.nn.sigmoid(g_lo), gup_ref[...], HIGHEST)
    kk = k * kk_ref[...]
    ss = _dot(kk * kk, hsum_ref[...], HIGHEST)
    kk = kk / jnp.maximum(jnp.sqrt(ss), 1e-12)
    k = k * (1.0 + (a - 1.0) * ka_ref[...])
    r_out[...] = r
    ld_out[...] = log_decay
    k_out[...] = k
    v_out[...] = v
    kk_out[...] = kk
    b_out[...] = kk * a
    g_out[...] = g
    bonus_out[...] = _dot(r * k * rk_ref[...], hsum_ref[...], HIGHEST) * v


def rwkv_prep(pc, mu, w0, w_up_pad, a0, a_up_pad, g_up, k_k, k_a, r_k, head_sum, seq, tm,
              v_first=None, v0=None, v_down=None, v_up=None):
    n, ccols = pc.shape
    cw = w0.shape[1]
    mix_value = v_first is not None
    const = lambda shape: pl.BlockSpec(shape, lambda i: (0,) * len(shape))
    in_specs = [pl.BlockSpec((tm, ccols), lambda i: (i, 0)),
                pl.BlockSpec((8, ccols), lambda i: (jnp.maximum(i * (tm // 8) - 1, 0), 0)),
                const((1, ccols)), const((1, cw)), const(w_up_pad.shape), const((1, cw)),
                const(a_up_pad.shape), const(g_up.shape), const((1, cw)), const((1, cw)),
                const((1, cw)), const(head_sum.shape)]
    args = [pc, pc, mu, w0, w_up_pad, a0, a_up_pad, g_up, k_k, k_a, r_k, head_sum]
    if mix_value:
        in_specs += [pl.BlockSpec((tm, cw), lambda i: (i, 0)), const((1, cw)),
                     const(v_down.shape), const(v_up.shape)]
        args += [v_first, v0, v_down, v_up]
    out = jax.ShapeDtypeStruct((n, cw), F32)
    return pl.pallas_call(
        functools.partial(_rwkv_prep_kernel, seq=seq, mix_value=mix_value),
        grid=(n // tm,),
        in_specs=in_specs,
        out_specs=[pl.BlockSpec((tm, cw), lambda i: (i, 0))] * 8,
        out_shape=[out] * 8,
        compiler_params=_params("parallel"),
        name="rwkv_prep",
    )(*args)


def _lane_head(shape):
    return lax.broadcasted_iota(jnp.int32, shape, 1) // C_HEAD_DIM


def _stack_masked(x, pack):
    head = _lane_head(x.shape)
    return jnp.concatenate([jnp.where(head == h, x, 0.0) for h in range(pack)], axis=0)


def _diag_blocks(full, pack):
    n = C_HEAD_DIM
    head = _lane_head((n, pack * n))
    out = full[0:n]
    for h in range(1, pack):
        out = jnp.where(head == h, full[h * n:(h + 1) * n], out)
    return out


def _mm(a, b):
    return _dot(a.astype(BF16), b.astype(BF16))


def _mm_nt(a, b):
    return _dot_nt(a.astype(BF16), b.astype(BF16))


def _mm_tn(a, b):
    return _dot_tn(a.astype(BF16), b.astype(BF16))


def _rwkv_chunk_kernel(r_ref, ld_ref, k_ref, v_ref, kk_ref, b_ref,
                       rq_out, y0_out, g_out, h_out, *, pack):
    tc, cw = r_ref.shape
    c = RWKV_CHUNK
    pw = pack * C_HEAD_DIM
    row = lax.broadcasted_iota(jnp.int32, (c, pw), 0)
    col = lax.broadcasted_iota(jnp.int32, (c, pw), 1) % C_HEAD_DIM
    incl = col <= row
    strict = col < row
    eye = (col == row).astype(F32)
    tril_ones = (lax.broadcasted_iota(jnp.int32, (c, c), 1)
                 <= lax.broadcasted_iota(jnp.int32, (c, c), 0)).astype(F32)

    units = [(slice(ci * c, (ci + 1) * c), slice(p * pw, (p + 1) * pw))
             for ci in range(tc // c) for p in range(cw // pw)]
    each = lambda fn, *lists: [fn(*args) for args in zip(*lists)]
    stack = lambda x: _stack_masked(x, pack)
    diag = lambda x: _diag_blocks(x, pack)

    cums = {}
    for rows, _ in units:
        if rows.start not in cums:
            cums[rows.start] = _dot(tril_ones, ld_ref[rows, :], HIGHEST)
    cum = [cums[rows.start][:, lanes] for rows, lanes in units]
    ld = [ld_ref[u] for u in units]
    v = [v_ref[u] for u in units]
    p_incl = each(jnp.exp, cum)
    p_inv = each(lambda x: jnp.exp(-x), cum)
    p_prev = each(lambda x, y: jnp.exp(x - y), cum, ld)
    r_t = each(lambda u, s: r_ref[u] * s, units, p_incl)
    kk_t = each(lambda u, s: kk_ref[u] * s, units, p_prev)
    b_t = each(lambda u, s: b_ref[u] * s, units, p_inv)
    k_t = each(lambda u, s: k_ref[u] * s, units, p_inv)

    lhs = each(lambda x, y: jnp.concatenate([x, y], axis=0).astype(BF16), kk_t, r_t)
    with_b = each(lambda x, y: _dot_nt(x, stack(y).astype(BF16)), lhs, b_t)
    with_k = each(lambda x, y: _dot_nt(x, stack(y).astype(BF16)), lhs, k_t)
    a_b = each(lambda x: jnp.where(strict, x[:c], 0.0), with_b)
    a_rb = each(lambda x: jnp.where(incl, x[c:], 0.0).astype(BF16), with_b)
    a_kk = each(lambda x: jnp.concatenate([jnp.where(strict, x[:c], 0.0),
                                           jnp.where(incl, x[c:], 0.0)], axis=0), with_k)
    av = each(lambda x, y: _mm(x, stack(y)), a_kk, v)

    lower_left = lambda m: (((row // (2 * m)) == (col // (2 * m)))
                            & ((row // m) % 2 == 1) & ((col // m) % 2 == 0))
    t_inv = each(lambda x: eye - jnp.where(lower_left(1), x, 0.0), a_b)
    m = 2
    while m < c:
        mask = lower_left(m)
        xl = each(lambda x, y: _mm(x, stack(jnp.where(mask, y, 0.0))), t_inv, a_b)
        t_inv = each(lambda x, y: x - _mm(y, stack(x)), t_inv, xl)
        m *= 2

    t_bf = each(lambda x: x.astype(BF16), t_inv)
    w = each(lambda x, y: _dot(x, stack(y).astype(BF16)), t_bf, kk_t)
    u0 = each(lambda x, y: _dot(x, stack(y[:c]).astype(BF16)), t_bf, av)
    for i, u in enumerate(units):
        rq_out[u] = r_t[i] - _dot(a_rb[i], stack(w[i]).astype(BF16))
        y0_out[u] = av[i][c:] - _dot(a_rb[i], stack(u0[i]).astype(BF16))
    for i, u in enumerate(units):
        p_end = p_incl[i][c - 1:c, :]
        g_out[u] = (eye - diag(_mm_tn(w[i], b_t[i]))) * p_end
        h_out[u] = (diag(_mm_tn(v[i], k_t[i])) - diag(_mm_tn(u0[i], b_t[i]))) * p_end


def rwkv_chunk(r, ld, k, v, kk, b, tc):
    n, cw = r.shape
    spec = pl.BlockSpec((tc, cw), lambda i: (i, 0))
    out = jax.ShapeDtypeStruct((n, cw), F32)
    return pl.pallas_call(
        functools.partial(_rwkv_chunk_kernel, pack=RWKV_PACK),
        grid=(n // tc,),
        in_specs=[spec] * 6,
        out_specs=[spec] * 4,
        out_shape=[out] * 4,
        compiler_params=_params("parallel"),
        name="rwkv_chunk",
    )(r, ld, k, v, kk, b)


def _rwkv_state_kernel(rq_ref, y0_ref, g_ref, h_ref, y_out, state_scr, *, pack):
    bn, tc, cw = rq_ref.shape
    c = RWKV_CHUNK
    pw = pack * C_HEAD_DIM
    groups = cw // pw

    @pl.when(pl.program_id(0) == 0)
    def _():
        state_scr[...] = jnp.zeros(state_scr.shape, F32)

    chains = [(bi, p, slice(p * pw, (p + 1) * pw)) for bi in range(bn) for p in range(groups)]
    state = [state_scr[bi, p] for bi, p, _ in chains]
    for ci in range(tc // c):
        rows = slice(ci * c, (ci + 1) * c)
        for i, (bi, p, lanes) in enumerate(chains):
            y_out[bi, rows, lanes] = _mm_nt(rq_ref[bi, rows, lanes], state[i]) + y0_ref[bi, rows, lanes]
        for i, (bi, p, lanes) in enumerate(chains):
            g_bd = _stack_masked(g_ref[bi, rows, lanes], pack)
            h_bd = _stack_masked(h_ref[bi, rows, lanes], pack)
            state[i] = _mm(state[i], g_bd) + h_bd
    for i, (bi, p, _) in enumerate(chains):
        state_scr[bi, p] = state[i]


def rwkv_state(rq, y0, g, h, tc):
    bn, s, cw = rq.shape
    pw = RWKV_PACK * C_HEAD_DIM
    spec = pl.BlockSpec((bn, tc, cw), lambda t: (0, t, 0))
    return pl.pallas_call(
        functools.partial(_rwkv_state_kernel, pack=RWKV_PACK),
        grid=(s // tc,),
        in_specs=[spec] * 4,
        out_specs=spec,
        out_shape=jax.ShapeDtypeStruct(rq.shape, F32),
        scratch_shapes=[pltpu.VMEM((bn, cw // pw, pw, pw), F32)],
        compiler_params=_params("arbitrary"),
        name="rwkv_state",
    )(rq, y0, g, h)


def _merge_kernel(x_ref, ya_ref, yb_ref, yc_ref, bonus_ref, g_ref, pg_ref, gb_ref, lng_ref, lnb_ref,
                  hsum_ref, pa_ref, pb_ref, pc_ref, wo_ref, o_ref):
    d = x_ref.shape[1]
    y = yc_ref[...]
    inv_n = 1.0 / C_HEAD_DIM
    mu = _dot(y, hsum_ref[...], HIGHEST) * inv_n
    yc = y - mu
    var = _dot(yc * yc, hsum_ref[...], HIGHEST) * inv_n
    yc = yc * lax.rsqrt(var + GN_EPS) * lng_ref[...] + lnb_ref[...]
    yc = (yc + bonus_ref[...]) * g_ref[...]
    branches = (_dot(ya_ref[...].astype(BF16), pa_ref[...]),
                _dot(yb_ref[...].astype(BF16), pb_ref[...]),
                _dot(yc.astype(BF16), pc_ref[...]))
    merged = jnp.zeros_like(branches[0])
    for bi in range(N_BRANCH):
        gate = jax.nn.sigmoid(pg_ref[:, bi * d:(bi + 1) * d] + gb_ref[bi:bi + 1, :])
        merged = merged + gate * branches[bi]
    o_ref[...] = x_ref[...] + _dot(merged.astype(BF16), wo_ref[...])


def merge_out(x, ya, yb, yc, bonus, g, pg, gate_bias, lnx_g, lnx_b, head_sum, p_a, p_b, p_c, w_out, tm):
    n, d = x.shape
    tile = lambda a: pl.BlockSpec((tm, a.shape[1]), lambda i: (i, 0))
    const = lambda a: pl.BlockSpec(a.shape, lambda i: (0, 0))
    return pl.pallas_call(
        _merge_kernel,
        grid=(n // tm,),
        in_specs=[tile(x), tile(ya), tile(yb), tile(yc), tile(bonus), tile(g), tile(pg),
                  const(gate_bias), const(lnx_g), const(lnx_b), const(head_sum),
                  const(p_a), const(p_b), const(p_c), const(w_out)],
        out_specs=pl.BlockSpec((tm, d), lambda i: (i, 0)),
        out_shape=jax.ShapeDtypeStruct((n, d), F32),
        compiler_params=_params("parallel"),
        name="merge_out",
    )(x, ya, yb, yc, bonus, g, pg, gate_bias, lnx_g, lnx_b, head_sum, p_a, p_b, p_c, w_out)


def _ffn_kernel(x_ref, g_ref, wg_ref, wu_ref, wd_ref, gf_ref, o_ref, h_scr, acc_scr, *, final_norm):
    j = pl.program_id(1)

    @pl.when(j == 0)
    def _():
        x = x_ref[...]
        h_scr[...] = _rmsnorm(x, g_ref[...]).astype(BF16)
        acc_scr[...] = x

    h = h_scr[...]
    act = jax.nn.silu(_dot(h, wg_ref[...])) * _dot(h, wu_ref[...])
    acc_scr[...] += _dot(act.astype(BF16), wd_ref[...])

    @pl.when(j == pl.num_programs(1) - 1)
    def _():
        y = acc_scr[...]
        o_ref[...] = _rmsnorm(y, gf_ref[...]) if final_norm else y


def ffn(x, g, w_gate_up, w_down, g_final, tm, tf, final_norm):
    n, d = x.shape
    dff = w_down.shape[0]
    nf = dff // tf
    return pl.pallas_call(
        functools.partial(_ffn_kernel, final_norm=final_norm),
        grid=(n // tm, nf),
        in_specs=[pl.BlockSpec((tm, d), lambda i, j: (i, 0)),
                  pl.BlockSpec((1, d), lambda i, j: (0, 0)),
                  pl.BlockSpec((d, tf), lambda i, j: (0, j)),
                  pl.BlockSpec((d, tf), lambda i, j: (0, j + nf)),
                  pl.BlockSpec((tf, d), lambda i, j: (j, 0)),
                  pl.BlockSpec((1, d), lambda i, j: (0, 0))],
        out_specs=pl.BlockSpec((tm, d), lambda i, j: (i, 0)),
        out_shape=jax.ShapeDtypeStruct((n, d), F32),
        scratch_shapes=[pltpu.VMEM((tm, d), BF16), pltpu.VMEM((tm, d), F32)],
        compiler_params=_params("parallel", "arbitrary"),
        name="ffn",
    )(x, g, w_gate_up, w_gate_up, w_down, g_final)


def _pad_cols(w, mult):
    pad = (-w.shape[1]) % mult
    return jnp.pad(w, ((0, 0), (0, pad))) if pad else w


def _largest_tile(total, cap):
    t = cap
    while total % t:
        t //= 2
    return t


def kernel(x, norm_mix, w_in, gate_bias, a_ln_g, a_ln_b, a_w_s, a_b_s, b_f_bias, c_mu, c_w0, c_w_up, c_a0, c_a_up, c_g_up, c_k_k, c_k_a, c_r_k, c_lnx_g, c_lnx_b, c_v0, c_v_down, c_v_up, p_a, p_b, p_c, w_out, norm_ffn, w_gate_up, w_down, norm_final):
    bn, s, d = x.shape
    depth = w_in.shape[0]
    n = bn * s
    a_width = a_ln_g.shape[1]
    b_width = B_HEADS * B_HEAD_DIM
    c_width = C_HEADS * C_HEAD_DIM
    a_cols = 2 * a_width
    b_cols = 3 * b_width + B_HEADS
    c_cols = 3 * c_width + C_DECAY_LORA + C_AAA_LORA + C_GATE_LORA
    dff = w_down.shape[1]
    assert s % 128 == 0 and dff % 256 == 0

    tm = _largest_tile(n, 512)
    t_attn = _largest_tile(s, 512)
    t_chunk = _largest_tile(s, 256)
    t_state = _largest_tile(s, 256)
    tf = dff // 2 if (dff // 2) % 128 == 0 else dff

    head_sum = jnp.kron(jnp.eye(C_HEADS, dtype=F32), jnp.ones((C_HEAD_DIM, C_HEAD_DIM), F32))
    row2 = lambda p: p.reshape(1, -1)

    xf = x.reshape(n, d)
    v_first = None
    for l in range(depth):
        w_l = w_in[l].astype(BF16)
        w_a = w_l[:, :a_cols]
        w_b = _pad_cols(w_l[:, a_cols:a_cols + b_cols], 128)
        w_c = w_l[:, a_cols + b_cols:a_cols + b_cols + c_cols]
        w_g = w_l[:, a_cols + b_cols + c_cols:]
        g_mix = row2(norm_mix[l])
        pa = norm_matmul(xf, g_mix, w_a, tm)
        pb = norm_matmul(xf, g_mix, w_b, tm)
        pc = norm_matmul(xf, g_mix, w_c, tm)
        pg = norm_matmul(xf, g_mix, w_g, tm)

        b_s_full = jnp.repeat(a_b_s[l].T, a_width // A_GROUPS, axis=1)
        ya = gmlp(pa, row2(a_ln_g[l]), row2(a_ln_b[l]), a_w_s[l], b_s_full, _largest_tile(n, 256))

        f_bias_row = jnp.pad(b_f_bias[l], (0, 128 - B_HEADS)).reshape(1, 128)
        qa, ka, vt = fox_prep(pb, f_bias_row, s, tm)
        yb = fox_attention(qa, ka, vt, bn, s, t_attn)

        zeros_lora = jnp.zeros((C_AAA_LORA, c_width), F32)
        w_up_pad = jnp.concatenate([c_w_up[l], zeros_lora], axis=0)
        a_up_pad = jnp.concatenate([jnp.zeros((C_DECAY_LORA, c_width), F32), c_a_up[l]], axis=0)
        mix = {} if l == 0 else dict(v_first=v_first, v0=row2(c_v0[l - 1]),
                                     v_down=c_v_down[l - 1], v_up=c_v_up[l - 1])
        r_c, ld_c, k_c, v_c, kk_c, b_c, g_c, bonus_c = rwkv_prep(
            pc, row2(c_mu[l]), row2(c_w0[l]), w_up_pad, row2(c_a0[l]), a_up_pad, c_g_up[l],
            row2(c_k_k[l]), row2(c_k_a[l]), row2(c_r_k[l]), head_sum, s, tm, **mix)
        if l == 0:
            v_first = v_c
        rq, y0, g_mat, h_mat = rwkv_chunk(r_c, ld_c, k_c, v_c, kk_c, b_c, t_chunk)
        per_batch = lambda t: t.reshape(bn, s, c_width)
        yc = rwkv_state(per_batch(rq), per_batch(y0), per_batch(g_mat), per_batch(h_mat), t_state)
        yc = yc.reshape(n, c_width)

        xf = merge_out(xf, ya, yb, yc, bonus_c, g_c, pg, gate_bias[l], row2(c_lnx_g[l]),
                       row2(c_lnx_b[l]), head_sum, p_a[l].astype(BF16), p_b[l].astype(BF16),
                       p_c[l].astype(BF16), w_out[l].astype(BF16), tm)
        xf = ffn(xf, row2(norm_ffn[l]), w_gate_up[l].astype(BF16), w_down[l].astype(BF16),
                 row2(norm_final), tm, tf, final_norm=(l == depth - 1))
    return xf.reshape(bn, s, d)
```

```python
import functools

import jax
import jax.numpy as jnp
from jax import lax
from jax.experimental import pallas as pl
from jax.experimental.pallas import tpu as pltpu

F32 = jnp.float32
BF16 = jnp.bfloat16

V7X_VMEM_LIMIT_BYTES = 56 * 1024 * 1024

NORM_EPS = 1e-6
LN_EPS = 1e-5
GN_EPS = 64e-5

N_BRANCH = 3
A_GROUPS = 4
A_CHUNK = 128
B_HEADS = 8
B_HEAD_DIM = 64
C_HEADS = 8
C_HEAD_DIM = 64
C_DECAY_LORA = 64
C_AAA_LORA = 64
C_GATE_LORA = 128
LOG2_E = 1.4426950408889634
FOX_HEAD_GROUP = 4
FOX_V_ROWS = 80
RWKV_CHUNK = 64
RWKV_PACK = 2


def _params(*semantics):
    return pltpu.CompilerParams(dimension_semantics=semantics,
                                vmem_limit_bytes=V7X_VMEM_LIMIT_BYTES)


def _rmsnorm(x, g):
    return x * lax.rsqrt(jnp.mean(x * x, axis=-1, keepdims=True) + NORM_EPS) * g


def _dot(a, b, precision=None):
    return jnp.dot(a, b, preferred_element_type=F32, precision=precision)


def _dot_nt(a, b, precision=None):
    return lax.dot_general(a, b, (((1,), (1,)), ((), ())),
                           preferred_element_type=F32, precision=precision)


def _bf16_terms(x, n):
    terms = []
    for _ in range(n - 1):
        t = x.astype(BF16)
        terms.append(t)
        x = x - t.astype(F32)
    terms.append(x.astype(BF16))
    return terms


def _dot_exact_rhs(x, m, n_terms):
    return sum(_dot(t, m) for t in _bf16_terms(x, n_terms))


def _dot_exact_lhs(m, x, n_terms):
    return sum(_dot(m, t) for t in _bf16_terms(x, n_terms))


def _dot_3pass(a, b):
    a_hi, a_lo = _bf16_terms(a, 2)
    b_hi, b_lo = _bf16_terms(b, 2)
    return _dot(a_hi, b_hi) + (_dot(a_lo, b_hi) + _dot(a_hi, b_lo))


def _dot_tn(a, b, precision=None):
    return lax.dot_general(a, b, (((0,), (0,)), ((), ())),
                           preferred_element_type=F32, precision=precision)


def _norm_matmul_kernel(x_ref, g_ref, w_ref, o_ref):
    h = _rmsnorm(x_ref[...], g_ref[...])
    o_ref[...] = _dot(h.astype(BF16), w_ref[...])


def norm_matmul(x, g, w, tm):
    n, d = x.shape
    cols = w.shape[1]
    return pl.pallas_call(
        _norm_matmul_kernel,
        grid=(n // tm,),
        in_specs=[pl.BlockSpec((tm, d), lambda i: (i, 0)),
                  pl.BlockSpec((1, d), lambda i: (0, 0)),
                  pl.BlockSpec((d, cols), lambda i: (0, 0))],
        out_specs=pl.BlockSpec((tm, cols), lambda i: (i, 0)),
        out_shape=jax.ShapeDtypeStruct((n, cols), F32),
        compiler_params=_params("parallel"),
        name="norm_matmul",
    )(x, g, w)


def _gmlp_kernel(pa_ref, lng_ref, lnb_ref, ws_ref, bs_ref, o_ref, *, width):
    tm = pa_ref.shape[0]
    gd = width // A_GROUPS
    row = lax.broadcasted_iota(jnp.int32, (A_CHUNK, A_CHUNK), 0)
    col = lax.broadcasted_iota(jnp.int32, (A_CHUNK, A_CHUNK), 1)
    causal = col <= row
    for c in range(tm // A_CHUNK):
        rows = slice(c * A_CHUNK, (c + 1) * A_CHUNK)
        act = jax.nn.gelu(pa_ref[rows, :])
        u = act[:, :width]
        v = act[:, width:]
        mu = jnp.mean(v, axis=-1, keepdims=True)
        var = jnp.mean(jnp.square(v - mu), axis=-1, keepdims=True)
        v = (v - mu) * lax.rsqrt(var + LN_EPS) * lng_ref[...] + lnb_ref[...]
        v = v.astype(BF16)
        for g in range(A_GROUPS):
            cols = slice(g * gd, (g + 1) * gd)
            w = jnp.where(causal, ws_ref[g], 0.0).astype(BF16)
            mixed = _dot(w, v[:, cols]) + bs_ref[:, cols]
            o_ref[rows, cols] = u[:, cols] * mixed


def gmlp(pa, ln_g, ln_b, w_s, b_s_full, tm):
    n, two_w = pa.shape
    width = two_w // 2
    return pl.pallas_call(
        functools.partial(_gmlp_kernel, width=width),
        grid=(n // tm,),
        in_specs=[pl.BlockSpec((tm, two_w), lambda i: (i, 0)),
                  pl.BlockSpec((1, width), lambda i: (0, 0)),
                  pl.BlockSpec((1, width), lambda i: (0, 0)),
                  pl.BlockSpec((A_GROUPS, A_CHUNK, A_CHUNK), lambda i: (0, 0, 0)),
                  pl.BlockSpec((A_CHUNK, width), lambda i: (0, 0))],
        out_specs=pl.BlockSpec((tm, width), lambda i: (i, 0)),
        out_shape=jax.ShapeDtypeStruct((n, width), F32),
        compiler_params=_params("parallel"),
        name="gmlp",
    )(pa, ln_g, ln_b, w_s, b_s_full)


def _one_hot(shape, hit):
    r = lax.broadcasted_iota(jnp.int32, shape, 0)
    c = lax.broadcasted_iota(jnp.int32, shape, 1)
    return jnp.where(hit(r, c), 1.0, 0.0).astype(BF16)


def _fox_prep_kernel(pb_ref, fb_ref, qa_ref, ka_ref, vt_ref, carry_scr, *, seq, scale):
    tm = pb_ref.shape[0]
    hd = B_HEAD_DIM
    width = B_HEADS * hd
    x = pb_ref[...]

    @pl.when((pl.program_id(0) * tm) % seq == 0)
    def _():
        carry_scr[...] = jnp.zeros(carry_scr.shape, F32)

    log_f = jax.nn.log_sigmoid(x[:, 3 * width:3 * width + 128] + fb_ref[...]) * LOG2_E
    tril = (lax.broadcasted_iota(jnp.int32, (tm, tm), 1)
            <= lax.broadcasted_iota(jnp.int32, (tm, tm), 0)).astype(BF16)
    cum = _dot_exact_lhs(tril, log_f, 3) + carry_scr[...]
    carry_scr[...] = cum[tm - 1:tm, :]
    neg = -cum
    c_hi = neg.astype(BF16)
    rest = neg - c_hi.astype(F32)
    c_mid = rest.astype(BF16)
    c_lo = (rest - c_mid.astype(F32)).astype(BF16)
    c_split = jnp.concatenate([c_hi, c_mid, c_lo], axis=1)

    lane = lax.broadcasted_iota(jnp.int32, (tm, 128), 1)
    ones_lanes = (lane >= hd) & (lane < hd + 3)
    v_rows_n = vt_ref.shape[2]
    row_t = lax.broadcasted_iota(jnp.int32, (v_rows_n, tm), 0)
    for h in range(B_HEADS):
        lanes = slice((h // 2) * 128, (h // 2) * 128 + 128)
        off = (h % 2) * hd
        to_low = _one_hot((128, 128), lambda r, c: (c < hd) & (r == c + off))
        place_c = _one_hot((384, 128), lambda r, c: (r % 128 == h) & (c == hd + r // 128))
        v_rows = _one_hot((v_rows_n, 128), lambda r, c: (r < hd) & (c == r + off))
        q = (x[:, lanes] * (scale * LOG2_E)).astype(BF16)
        k = x[:, width:2 * width][:, lanes].astype(BF16)
        v = x[:, 2 * width:3 * width][:, lanes].astype(BF16)
        qa_ref[h] = jnp.where(ones_lanes, 1.0, _dot(q, to_low)).astype(BF16)
        ka_ref[h] = (_dot(k, to_low) + _dot(c_split, place_c)).astype(BF16)
        vt_ref[h, 0] = jnp.where(row_t == hd, 1.0, _dot_nt(v_rows, v)).astype(BF16)


def fox_prep(pb, f_bias_row, seq, tm):
    n = pb.shape[0]
    return pl.pallas_call(
        functools.partial(_fox_prep_kernel, seq=seq, scale=B_HEAD_DIM ** -0.5),
        grid=(n // tm,),
        in_specs=[pl.BlockSpec((tm, pb.shape[1]), lambda i: (i, 0)),
                  pl.BlockSpec((1, 128), lambda i: (0, 0))],
        out_specs=[pl.BlockSpec((B_HEADS, tm, 128), lambda i: (0, i, 0)),
                   pl.BlockSpec((B_HEADS, tm, 128), lambda i: (0, i, 0)),
                   pl.BlockSpec((B_HEADS, 1, FOX_V_ROWS, tm), lambda i: (0, i, 0, 0))],
        out_shape=[jax.ShapeDtypeStruct((B_HEADS, n, 128), BF16),
                   jax.ShapeDtypeStruct((B_HEADS, n, 128), BF16),
                   jax.ShapeDtypeStruct((B_HEADS, n // tm, FOX_V_ROWS, tm), BF16)],
        scratch_shapes=[pltpu.VMEM((1, 128), F32)],
        compiler_params=_params("arbitrary"),
        name="fox_prep",
    )(pb, f_bias_row)


def _fox_kernel(qa_ref, ka_ref, vt_ref, o_ref, m_scr, acc_scr, s_scr):
    i = pl.program_id(2)
    heads, tq, _ = qa_ref.shape
    tk = vt_ref.shape[3]
    hd = B_HEAD_DIM
    m_scr[...] = jnp.full(m_scr.shape, -jnp.inf, F32)
    acc_scr[...] = jnp.zeros(acc_scr.shape, F32)

    hs = range(heads)

    def logits(j, slot):
        start = pl.multiple_of(j * tk, tk)
        for h in hs:
            s_scr[slot, h] = _dot_nt(ka_ref[h, pl.ds(start, tk), :], qa_ref[h])

    def consume(slot, j, masked):
        s = [s_scr[slot, h] for h in hs]
        if masked:
            key = lax.broadcasted_iota(jnp.int32, (tk, tq), 0)
            qry = lax.broadcasted_iota(jnp.int32, (tk, tq), 1)
            s = [jnp.where(key <= qry, x, -jnp.inf) for x in s]
        m_prev = [m_scr[h] for h in hs]
        m_new = [jnp.maximum(m_prev[h], jnp.max(s[h], axis=0, keepdims=True)) for h in hs]
        p = [jnp.exp2((s[h] - m_new[h]).astype(BF16)) for h in hs]
        pv = [_dot(vt_ref[h, j], p[h]) for h in hs]
        for h in hs:
            acc_scr[h] = jnp.exp2(m_prev[h] - m_new[h]) * acc_scr[h] + pv[h]
            m_scr[h] = m_new[h]

    logits(0, 0)

    def body(jj, carry):
        a = 2 * jj
        logits(a + 1, 1)
        consume(0, a, False)
        logits(a + 2, 0)
        consume(1, a + 1, False)
        return carry

    lax.fori_loop(0, i // 2, body, 0)

    @pl.when(i % 2 == 0)
    def _():
        consume(0, i, True)

    @pl.when(i % 2 == 1)
    def _():
        logits(i, 1)
        consume(0, i - 1, False)
        consume(1, i, True)

    out_t = [acc_scr[h, 0:hd, :] / acc_scr[h, hd:hd + 1, :] for h in range(heads)]
    o_ref[...] = jnp.concatenate(out_t, axis=0).T


def fox_attention(qa, ka, vt, bn, seq, tq, group):
    heads, n, _ = qa.shape
    rows = vt.shape[2]
    nq = seq // tq
    return pl.pallas_call(
        _fox_kernel,
        grid=(bn, heads // group, nq),
        in_specs=[pl.BlockSpec((group, tq, 128), lambda bi, hi, i: (hi, bi * nq + i, 0)),
                  pl.BlockSpec((group, seq, 128), lambda bi, hi, i: (hi, bi, 0)),
                  pl.BlockSpec((group, nq, rows, tq), lambda bi, hi, i: (hi, bi, 0, 0))],
        out_specs=pl.BlockSpec((tq, group * B_HEAD_DIM), lambda bi, hi, i: (bi * nq + i, hi)),
        out_shape=jax.ShapeDtypeStruct((n, heads * B_HEAD_DIM), F32),
        scratch_shapes=[pltpu.VMEM((group, 1, tq), F32), pltpu.VMEM((group, rows, tq), F32),
                        pltpu.VMEM((2, group, tq, tq), F32)],
        compiler_params=_params("parallel", "parallel", "arbitrary"),
        name="fox_attention",
    )(qa, ka, vt)


def _rwkv_prep_kernel(*refs, seq, mix_value):
    if mix_value:
        (pc_ref, prev_ref, mu_ref, w0_ref, wup_ref, a0_ref, aup_ref, gup_ref, kk_ref, ka_ref,
         rk_ref, hsum_ref, vfirst_ref, v0_ref, vdown_ref, vup_ref,
         r_out, ld_out, k_out, v_out, kk_out, b_out, g_out, bonus_out) = refs
    else:
        (pc_ref, prev_ref, mu_ref, w0_ref, wup_ref, a0_ref, aup_ref, gup_ref, kk_ref, ka_ref,
         rk_ref, hsum_ref,
         r_out, ld_out, k_out, v_out, kk_out, b_out, g_out, bonus_out) = refs
    tm = pc_ref.shape[0]
    cw = r_out.shape[1]
    pc = pc_ref[...]
    at_seq_start = (pl.program_id(0) * tm) % seq == 0
    prev_row = jnp.where(at_seq_start, 0.0, prev_ref[7:8, :])
    row = lax.broadcasted_iota(jnp.int32, pc.shape, 0)
    shifted = jnp.where(row == 0, prev_row, pltpu.roll(pc, 1, axis=0))
    pc = pc + (shifted - pc) * mu_ref[...]

    r = pc[:, 0:cw]
    k = pc[:, cw:2 * cw]
    v = pc[:, 2 * cw:3 * cw]
    lora = pc[:, 3 * cw:3 * cw + C_DECAY_LORA + C_AAA_LORA]
    g_lo = pc[:, 3 * cw + C_DECAY_LORA + C_AAA_LORA:]
    if mix_value:
        lam = jax.nn.sigmoid(v0_ref[...] + _mm(_mm(v, vdown_ref[...]), vup_ref[...]))
        v = v + (vfirst_ref[...] - v) * lam
    w = -jax.nn.softplus(-(w0_ref[...] + _dot_3pass(jnp.tanh(lora), wup_ref[...]))) - 0.5
    log_decay = -jnp.exp(w)
    a = jax.nn.sigmoid(a0_ref[...] + _mm(lora, aup_ref[...]))
    g = _mm(jax.nn.sigmoid(g_lo), gup_ref[...])
    kk = k * kk_ref[...]
    ss = _dot_exact_rhs(kk * kk, hsum_ref[...], 2)
    kk = kk / jnp.maximum(jnp.sqrt(ss), 1e-12)
    k = k * (1.0 + (a - 1.0) * ka_ref[...])
    r_out[...] = r
    ld_out[...] = log_decay
    k_out[...] = k
    v_out[...] = v
    kk_out[...] = kk
    b_out[...] = kk * a
    g_out[...] = g
    bonus_out[...] = _dot_exact_rhs(r * k * rk_ref[...], hsum_ref[...], 2) * v


def rwkv_prep(pc, mu, w0, w_up_pad, a0, a_up_pad, g_up, k_k, k_a, r_k, head_sum, seq, tm,
              v_first=None, v0=None, v_down=None, v_up=None):
    n, ccols = pc.shape
    cw = w0.shape[1]
    mix_value = v_first is not None
    const = lambda shape: pl.BlockSpec(shape, lambda i: (0,) * len(shape))
    in_specs = [pl.BlockSpec((tm, ccols), lambda i: (i, 0)),
                pl.BlockSpec((8, ccols), lambda i: (jnp.maximum(i * (tm // 8) - 1, 0), 0)),
                const((1, ccols)), const((1, cw)), const(w_up_pad.shape), const((1, cw)),
                const(a_up_pad.shape), const(g_up.shape), const((1, cw)), const((1, cw)),
                const((1, cw)), const(head_sum.shape)]
    args = [pc, pc, mu, w0, w_up_pad, a0, a_up_pad, g_up, k_k, k_a, r_k, head_sum]
    if mix_value:
        in_specs += [pl.BlockSpec((tm, cw), lambda i: (i, 0)), const((1, cw)),
                     const(v_down.shape), const(v_up.shape)]
        args += [v_first, v0, v_down, v_up]
    out = jax.ShapeDtypeStruct((n, cw), F32)
    return pl.pallas_call(
        functools.partial(_rwkv_prep_kernel, seq=seq, mix_value=mix_value),
        grid=(n // tm,),
        in_specs=in_specs,
        out_specs=[pl.BlockSpec((tm, cw), lambda i: (i, 0))] * 8,
        out_shape=[out] * 8,
        compiler_params=_params("parallel"),
        name="rwkv_prep",
    )(*args)


def _lane_head(shape):
    return lax.broadcasted_iota(jnp.int32, shape, 1) // C_HEAD_DIM


def _stack_masked(x, pack):
    head = _lane_head(x.shape)
    return jnp.concatenate([jnp.where(head == h, x, 0.0) for h in range(pack)], axis=0)


def _diag_blocks(full, pack):
    n = C_HEAD_DIM
    head = _lane_head((n, pack * n))
    out = full[0:n]
    for h in range(1, pack):
        out = jnp.where(head == h, full[h * n:(h + 1) * n], out)
    return out


def _mm(a, b):
    return _dot(a.astype(BF16), b.astype(BF16))


def _mm_nt(a, b):
    return _dot_nt(a.astype(BF16), b.astype(BF16))


def _mm_tn(a, b):
    return _dot_tn(a.astype(BF16), b.astype(BF16))


def _rwkv_chunk_kernel(r_ref, ld_ref, k_ref, v_ref, kk_ref, b_ref,
                       rq_out, y0_out, g_out, h_out, *, pack):
    tc, cw = r_ref.shape
    c = RWKV_CHUNK
    pw = pack * C_HEAD_DIM
    row = lax.broadcasted_iota(jnp.int32, (c, pw), 0)
    col = lax.broadcasted_iota(jnp.int32, (c, pw), 1) % C_HEAD_DIM
    incl = col <= row
    strict = col < row
    eye = (col == row).astype(F32)
    tril_ones = (lax.broadcasted_iota(jnp.int32, (c, c), 1)
                 <= lax.broadcasted_iota(jnp.int32, (c, c), 0)).astype(BF16)

    units = [(slice(ci * c, (ci + 1) * c), slice(p * pw, (p + 1) * pw))
             for ci in range(tc // c) for p in range(cw // pw)]
    each = lambda fn, *lists: [fn(*args) for args in zip(*lists)]
    stack = lambda x: _stack_masked(x, pack)
    diag = lambda x: _diag_blocks(x, pack)

    cums = {}
    for rows, _ in units:
        if rows.start not in cums:
            cums[rows.start] = _dot_exact_lhs(tril_ones, ld_ref[rows, :], 3)
    cum = [cums[rows.start][:, lanes] for rows, lanes in units]
    ld = [ld_ref[u] for u in units]
    v = [v_ref[u] for u in units]
    p_incl = each(jnp.exp, cum)
    p_inv = each(lambda x: jnp.exp(-x), cum)
    p_prev = each(lambda x, y: jnp.exp(x - y), cum, ld)
    r_t = each(lambda u, s: r_ref[u] * s, units, p_incl)
    kk_t = each(lambda u, s: kk_ref[u] * s, units, p_prev)
    b_t = each(lambda u, s: b_ref[u] * s, units, p_inv)
    k_t = each(lambda u, s: k_ref[u] * s, units, p_inv)

    lhs = each(lambda x, y: jnp.concatenate([x, y], axis=0).astype(BF16), kk_t, r_t)
    with_b = each(lambda x, y: _dot_nt(x, stack(y).astype(BF16)), lhs, b_t)
    with_k = each(lambda x, y: _dot_nt(x, stack(y).astype(BF16)), lhs, k_t)
    a_b = each(lambda x: jnp.where(strict, x[:c], 0.0), with_b)
    a_rb = each(lambda x: jnp.where(incl, x[c:], 0.0).astype(BF16), with_b)
    a_kk = each(lambda x: jnp.concatenate([jnp.where(strict, x[:c], 0.0),
                                           jnp.where(incl, x[c:], 0.0)], axis=0), with_k)
    av = each(lambda x, y: _mm(x, stack(y)), a_kk, v)

    lower_left = lambda m: (((row // (2 * m)) == (col // (2 * m)))
                            & ((row // m) % 2 == 1) & ((col // m) % 2 == 0))
    t_inv = each(lambda x: eye - jnp.where(lower_left(1), x, 0.0), a_b)
    m = 2
    while m < c:
        mask = lower_left(m)
        xl = each(lambda x, y: _mm(x, stack(jnp.where(mask, y, 0.0))), t_inv, a_b)
        t_inv = each(lambda x, y: x - _mm(y, stack(x)), t_inv, xl)
        m *= 2

    t_bf = each(lambda x: x.astype(BF16), t_inv)
    w = each(lambda x, y: _dot(x, stack(y).astype(BF16)), t_bf, kk_t)
    u0 = each(lambda x, y: _dot(x, stack(y[:c]).astype(BF16)), t_bf, av)
    for i, u in enumerate(units):
        rq_out[u] = r_t[i] - _dot(a_rb[i], stack(w[i]).astype(BF16))
        y0_out[u] = av[i][c:] - _dot(a_rb[i], stack(u0[i]).astype(BF16))
    for i, u in enumerate(units):
        p_end = p_incl[i][c - 1:c, :]
        g_out[u] = (eye - diag(_mm_tn(w[i], b_t[i]))) * p_end
        h_out[u] = (diag(_mm_tn(v[i], k_t[i])) - diag(_mm_tn(u0[i], b_t[i]))) * p_end


def rwkv_chunk(r, ld, k, v, kk, b, tc):
    n, cw = r.shape
    spec = pl.BlockSpec((tc, cw), lambda i: (i, 0))
    out = jax.ShapeDtypeStruct((n, cw), F32)
    return pl.pallas_call(
        functools.partial(_rwkv_chunk_kernel, pack=RWKV_PACK),
        grid=(n // tc,),
        in_specs=[spec] * 6,
        out_specs=[spec] * 4,
        out_shape=[out] * 4,
        compiler_params=_params("parallel"),
        name="rwkv_chunk",
    )(r, ld, k, v, kk, b)


def _rwkv_state_kernel(rq_ref, y0_ref, g_ref, h_ref, y_out, state_scr, *, pack):
    bn, tc, cw = rq_ref.shape
    c = RWKV_CHUNK
    pw = pack * C_HEAD_DIM
    groups = cw // pw

    @pl.when(pl.program_id(0) == 0)
    def _():
        state_scr[...] = jnp.zeros(state_scr.shape, F32)

    chains = [(bi, p, slice(p * pw, (p + 1) * pw)) for bi in range(bn) for p in range(groups)]
    state = [state_scr[bi, p] for bi, p, _ in chains]
    for ci in range(tc // c):
        rows = slice(ci * c, (ci + 1) * c)
        for i, (bi, p, lanes) in enumerate(chains):
            y_out[bi, rows, lanes] = _mm_nt(rq_ref[bi, rows, lanes], state[i]) + y0_ref[bi, rows, lanes]
        for i, (bi, p, lanes) in enumerate(chains):
            g_bd = _stack_masked(g_ref[bi, rows, lanes], pack)
            h_bd = _stack_masked(h_ref[bi, rows, lanes], pack)
            state[i] = _mm(state[i], g_bd) + h_bd
    for i, (bi, p, _) in enumerate(chains):
        state_scr[bi, p] = state[i]


def rwkv_state(rq, y0, g, h, tc):
    bn, s, cw = rq.shape
    pw = RWKV_PACK * C_HEAD_DIM
    spec = pl.BlockSpec((bn, tc, cw), lambda t: (0, t, 0))
    return pl.pallas_call(
        functools.partial(_rwkv_state_kernel, pack=RWKV_PACK),
        grid=(s // tc,),
        in_specs=[spec] * 4,
        out_specs=spec,
        out_shape=jax.ShapeDtypeStruct(rq.shape, F32),
        scratch_shapes=[pltpu.VMEM((bn, cw // pw, pw, pw), F32)],
        compiler_params=_params("arbitrary"),
        name="rwkv_state",
    )(rq, y0, g, h)


def _merge_kernel(x_ref, ya_ref, yb_ref, yc_ref, bonus_ref, g_ref, pg_ref, gb_ref, lng_ref, lnb_ref,
                  hsum_ref, pa_ref, pb_ref, pc_ref, wo_ref, o_ref):
    d = x_ref.shape[1]
    y = yc_ref[...]
    inv_n = 1.0 / C_HEAD_DIM
    mu = _dot_exact_rhs(y, hsum_ref[...], 2) * inv_n
    yc = y - mu
    var = _dot_exact_rhs(yc * yc, hsum_ref[...], 2) * inv_n
    yc = yc * lax.rsqrt(var + GN_EPS) * lng_ref[...] + lnb_ref[...]
    yc = (yc + bonus_ref[...]) * g_ref[...]
    branches = (_dot(ya_ref[...].astype(BF16), pa_ref[...]),
                _dot(yb_ref[...].astype(BF16), pb_ref[...]),
                _dot(yc.astype(BF16), pc_ref[...]))
    merged = jnp.zeros_like(branches[0])
    for bi in range(N_BRANCH):
        gate = jax.nn.sigmoid(pg_ref[:, bi * d:(bi + 1) * d] + gb_ref[bi:bi + 1, :])
        merged = merged + gate * branches[bi]
    o_ref[...] = x_ref[...] + _dot(merged.astype(BF16), wo_ref[...])


def merge_out(x, ya, yb, yc, bonus, g, pg, gate_bias, lnx_g, lnx_b, head_sum, p_a, p_b, p_c, w_out, tm):
    n, d = x.shape
    tile = lambda a: pl.BlockSpec((tm, a.shape[1]), lambda i: (i, 0))
    const = lambda a: pl.BlockSpec(a.shape, lambda i: (0, 0))
    return pl.pallas_call(
        _merge_kernel,
        grid=(n // tm,),
        in_specs=[tile(x), tile(ya), tile(yb), tile(yc), tile(bonus), tile(g), tile(pg),
                  const(gate_bias), const(lnx_g), const(lnx_b), const(head_sum),
                  const(p_a), const(p_b), const(p_c), const(w_out)],
        out_specs=pl.BlockSpec((tm, d), lambda i: (i, 0)),
        out_shape=jax.ShapeDtypeStruct((n, d), F32),
        compiler_params=_params("parallel"),
        name="merge_out",
    )(x, ya, yb, yc, bonus, g, pg, gate_bias, lnx_g, lnx_b, head_sum, p_a, p_b, p_c, w_out)


def _ffn_kernel(x_ref, g_ref, wg_ref, wu_ref, wd_ref, gf_ref, o_ref, h_scr, acc_scr, *, final_norm):
    j = pl.program_id(1)

    @pl.when(j == 0)
    def _():
        x = x_ref[...]
        h_scr[...] = _rmsnorm(x, g_ref[...]).astype(BF16)
        acc_scr[...] = x

    h = h_scr[...]
    act = jax.nn.silu(_dot(h, wg_ref[...])) * _dot(h, wu_ref[...])
    acc_scr[...] += _dot(act.astype(BF16), wd_ref[...])

    @pl.when(j == pl.num_programs(1) - 1)
    def _():
        y = acc_scr[...]
        o_ref[...] = _rmsnorm(y, gf_ref[...]) if final_norm else y


def ffn(x, g, w_gate_up, w_down, g_final, tm, tf, final_norm):
    n, d = x.shape
    dff = w_down.shape[0]
    nf = dff // tf
    return pl.pallas_call(
        functools.partial(_ffn_kernel, final_norm=final_norm),
        grid=(n // tm, nf),
        in_specs=[pl.BlockSpec((tm, d), lambda i, j: (i, 0)),
                  pl.BlockSpec((1, d), lambda i, j: (0, 0)),
                  pl.BlockSpec((d, tf), lambda i, j: (0, j)),
                  pl.BlockSpec((d, tf), lambda i, j: (0, j + nf)),
                  pl.BlockSpec((tf, d), lambda i, j: (j, 0)),
                  pl.BlockSpec((1, d), lambda i, j: (0, 0))],
        out_specs=pl.BlockSpec((tm, d), lambda i, j: (i, 0)),
        out_shape=jax.ShapeDtypeStruct((n, d), F32),
        scratch_shapes=[pltpu.VMEM((tm, d), BF16), pltpu.VMEM((tm, d), F32)],
        compiler_params=_params("parallel", "arbitrary"),
        name="ffn",
    )(x, g, w_gate_up, w_gate_up, w_down, g_final)


def _pad_cols(w, mult):
    pad = (-w.shape[1]) % mult
    return jnp.pad(w, ((0, 0), (0, pad))) if pad else w


def _largest_tile(total, cap):
    t = cap
    while total % t:
        t //= 2
    return t


def kernel(x, norm_mix, w_in, gate_bias, a_ln_g, a_ln_b, a_w_s, a_b_s, b_f_bias, c_mu, c_w0, c_w_up, c_a0, c_a_up, c_g_up, c_k_k, c_k_a, c_r_k, c_lnx_g, c_lnx_b, c_v0, c_v_down, c_v_up, p_a, p_b, p_c, w_out, norm_ffn, w_gate_up, w_down, norm_final):
    bn, s, d = x.shape
    depth = w_in.shape[0]
    n = bn * s
    a_width = a_ln_g.shape[1]
    b_width = B_HEADS * B_HEAD_DIM
    c_width = C_HEADS * C_HEAD_DIM
    a_cols = 2 * a_width
    b_cols = 3 * b_width + B_HEADS
    c_cols = 3 * c_width + C_DECAY_LORA + C_AAA_LORA + C_GATE_LORA
    dff = w_down.shape[1]
    assert s % 128 == 0 and dff % 256 == 0

    tm = _largest_tile(n, 512)
    t_attn = _largest_tile(s, 512)
    t_chunk = _largest_tile(s, 256)
    t_state = _largest_tile(s, 256)
    tf = dff // 2 if (dff // 2) % 128 == 0 else dff

    head_sum = jnp.kron(jnp.eye(C_HEADS, dtype=BF16), jnp.ones((C_HEAD_DIM, C_HEAD_DIM), BF16))
    row2 = lambda p: p.reshape(1, -1)

    xf = x.reshape(n, d)
    v_first = None
    for l in range(depth):
        w_l = w_in[l].astype(BF16)
        w_a = w_l[:, :a_cols]
        w_b = _pad_cols(w_l[:, a_cols:a_cols + b_cols], 128)
        w_c = w_l[:, a_cols + b_cols:a_cols + b_cols + c_cols]
        w_g = w_l[:, a_cols + b_cols + c_cols:]
        g_mix = row2(norm_mix[l])
        pa = norm_matmul(xf, g_mix, w_a, tm)
        pb = norm_matmul(xf, g_mix, w_b, tm)
        pc = norm_matmul(xf, g_mix, w_c, tm)
        pg = norm_matmul(xf, g_mix, w_g, tm)

        b_s_full = jnp.repeat(a_b_s[l].T, a_width // A_GROUPS, axis=1)
        ya = gmlp(pa, row2(a_ln_g[l]), row2(a_ln_b[l]), a_w_s[l], b_s_full, _largest_tile(n, 256))

        f_bias_row = jnp.pad(b_f_bias[l], (0, 128 - B_HEADS)).reshape(1, 128)
        qa, ka, vt = fox_prep(pb, f_bias_row, s, tm)
        yb = fox_attention(qa, ka, vt, bn, s, tm, FOX_HEAD_GROUP)

        zeros_lora = jnp.zeros((C_AAA_LORA, c_width), F32)
        w_up_pad = jnp.concatenate([c_w_up[l], zeros_lora], axis=0)
        a_up_pad = jnp.concatenate([jnp.zeros((C_DECAY_LORA, c_width), F32), c_a_up[l]], axis=0)
        mix = {} if l == 0 else dict(v_first=v_first, v0=row2(c_v0[l - 1]),
                                     v_down=c_v_down[l - 1], v_up=c_v_up[l - 1])
        r_c, ld_c, k_c, v_c, kk_c, b_c, g_c, bonus_c = rwkv_prep(
            pc, row2(c_mu[l]), row2(c_w0[l]), w_up_pad, row2(c_a0[l]), a_up_pad, c_g_up[l],
            row2(c_k_k[l]), row2(c_k_a[l]), row2(c_r_k[l]), head_sum, s, tm, **mix)
        if l == 0:
            v_first = v_c
        rq, y0, g_mat, h_mat = rwkv_chunk(r_c, ld_c, k_c, v_c, kk_c, b_c, t_chunk)
        per_batch = lambda t: t.reshape(bn, s, c_width)
        yc = rwkv_state(per_batch(rq), per_batch(y0), per_batch(g_mat), per_batch(h_mat), t_state)
        yc = yc.reshape(n, c_width)

        xf = merge_out(xf, ya, yb, yc, bonus_c, g_c, pg, gate_bias[l], row2(c_lnx_g[l]),
                       row2(c_lnx_b[l]), head_sum, p_a[l].astype(BF16), p_b[l].astype(BF16),
                       p_c[l].astype(BF16), w_out[l].astype(BF16), tm)
        xf = ffn(xf, row2(norm_ffn[l]), w_gate_up[l].astype(BF16), w_down[l].astype(BF16),
                 row2(norm_final), tm, tf, final_norm=(l == depth - 1))
    return xf.reshape(bn, s, d)
```

```python
import functools

import jax
import jax.numpy as jnp
from jax import lax
from jax.experimental import pallas as pl
from jax.experimental.pallas import tpu as pltpu

F32 = jnp.float32
BF16 = jnp.bfloat16

V7X_VMEM_LIMIT_BYTES = 56 * 1024 * 1024

NORM_EPS = 1e-6
LN_EPS = 1e-5
GN_EPS = 64e-5

N_BRANCH = 3
A_GROUPS = 4
A_CHUNK = 128
B_HEADS = 8
B_HEAD_DIM = 64
C_HEADS = 8
C_HEAD_DIM = 64
C_DECAY_LORA = 64
C_AAA_LORA = 64
C_GATE_LORA = 128
LOG2_E = 1.4426950408889634
FOX_HEAD_GROUP = 4
FOX_V_ROWS = 80
RWKV_CHUNK = 64
RWKV_PACK = 2


def _params(*semantics):
    return pltpu.CompilerParams(dimension_semantics=semantics,
                                vmem_limit_bytes=V7X_VMEM_LIMIT_BYTES)


def _rmsnorm(x, g):
    return x * lax.rsqrt(jnp.mean(x * x, axis=-1, keepdims=True) + NORM_EPS) * g


def _dot(a, b, precision=None):
    return jnp.dot(a, b, preferred_element_type=F32, precision=precision)


def _dot_nt(a, b, precision=None):
    return lax.dot_general(a, b, (((1,), (1,)), ((), ())),
                           preferred_element_type=F32, precision=precision)


def _bf16_terms(x, n):
    terms = []
    for _ in range(n - 1):
        t = x.astype(BF16)
        terms.append(t)
        x = x - t.astype(F32)
    terms.append(x.astype(BF16))
    return terms


def _dot_exact_rhs(x, m, n_terms):
    return sum(_dot(t, m) for t in _bf16_terms(x, n_terms))


def _dot_exact_lhs(m, x, n_terms):
    return sum(_dot(m, t) for t in _bf16_terms(x, n_terms))


def _dot_3pass(a, b):
    a_hi, a_lo = _bf16_terms(a, 2)
    b_hi, b_lo = _bf16_terms(b, 2)
    return _dot(a_hi, b_hi) + (_dot(a_lo, b_hi) + _dot(a_hi, b_lo))


def _dot_tn(a, b, precision=None):
    return lax.dot_general(a, b, (((0,), (0,)), ((), ())),
                           preferred_element_type=F32, precision=precision)


def _const_spec(shape):
    return pl.BlockSpec(shape, lambda *_: (0,) * len(shape), pipeline_mode=pl.Buffered(1))


def _project(x_ref, g_ref, w_ref):
    return _dot(_rmsnorm(x_ref[...], g_ref[...]).astype(BF16), w_ref[...])


def _gmlp_kernel(x_ref, gn_ref, w_ref, lng_ref, lnb_ref, ws_ref, bs_ref, o_ref, *, width):
    tm = x_ref.shape[0]
    gd = width // A_GROUPS
    pa = _project(x_ref, gn_ref, w_ref)
    row = lax.broadcasted_iota(jnp.int32, (A_CHUNK, A_CHUNK), 0)
    col = lax.broadcasted_iota(jnp.int32, (A_CHUNK, A_CHUNK), 1)
    causal = col <= row
    for c in range(tm // A_CHUNK):
        rows = slice(c * A_CHUNK, (c + 1) * A_CHUNK)
        act = jax.nn.gelu(pa[rows, :])
        u = act[:, :width]
        v = act[:, width:]
        mu = jnp.mean(v, axis=-1, keepdims=True)
        var = jnp.mean(jnp.square(v - mu), axis=-1, keepdims=True)
        v = (v - mu) * lax.rsqrt(var + LN_EPS) * lng_ref[...] + lnb_ref[...]
        v = v.astype(BF16)
        for g in range(A_GROUPS):
            cols = slice(g * gd, (g + 1) * gd)
            w = jnp.where(causal, ws_ref[g], 0.0).astype(BF16)
            mixed = _dot(w, v[:, cols]) + bs_ref[:, cols]
            o_ref[rows, cols] = u[:, cols] * mixed


def gmlp(x, g_norm, w_a, ln_g, ln_b, w_s, b_s_full, tm):
    n, d = x.shape
    width = w_a.shape[1] // 2
    return pl.pallas_call(
        functools.partial(_gmlp_kernel, width=width),
        grid=(n // tm,),
        in_specs=[pl.BlockSpec((tm, d), lambda i: (i, 0)),
                  _const_spec((1, d)), _const_spec(w_a.shape),
                  _const_spec((1, width)), _const_spec((1, width)),
                  _const_spec((A_GROUPS, A_CHUNK, A_CHUNK)), _const_spec((A_CHUNK, width))],
        out_specs=pl.BlockSpec((tm, width), lambda i: (i, 0)),
        out_shape=jax.ShapeDtypeStruct((n, width), F32),
        compiler_params=_params("parallel"),
        name="gmlp",
    )(x, g_norm, w_a, ln_g, ln_b, w_s, b_s_full)


def _one_hot(shape, hit):
    r = lax.broadcasted_iota(jnp.int32, shape, 0)
    c = lax.broadcasted_iota(jnp.int32, shape, 1)
    return jnp.where(hit(r, c), 1.0, 0.0).astype(BF16)


def _fox_prep_kernel(x_ref, gn_ref, w_ref, fb_ref, qa_ref, ka_ref, vt_ref, carry_scr, *, seq, scale):
    tm = x_ref.shape[0]
    hd = B_HEAD_DIM
    width = B_HEADS * hd
    x = _project(x_ref, gn_ref, w_ref)

    @pl.when((pl.program_id(0) * tm) % seq == 0)
    def _():
        carry_scr[...] = jnp.zeros(carry_scr.shape, F32)

    log_f = jax.nn.log_sigmoid(x[:, 3 * width:3 * width + 128] + fb_ref[...]) * LOG2_E
    tril = (lax.broadcasted_iota(jnp.int32, (tm, tm), 1)
            <= lax.broadcasted_iota(jnp.int32, (tm, tm), 0)).astype(BF16)
    cum = _dot_exact_lhs(tril, log_f, 3) + carry_scr[...]
    carry_scr[...] = cum[tm - 1:tm, :]
    neg = -cum
    c_hi = neg.astype(BF16)
    rest = neg - c_hi.astype(F32)
    c_mid = rest.astype(BF16)
    c_lo = (rest - c_mid.astype(F32)).astype(BF16)
    c_split = jnp.concatenate([c_hi, c_mid, c_lo], axis=1)

    lane = lax.broadcasted_iota(jnp.int32, (tm, 128), 1)
    ones_lanes = (lane >= hd) & (lane < hd + 3)
    v_rows_n = vt_ref.shape[2]
    row_t = lax.broadcasted_iota(jnp.int32, (v_rows_n, tm), 0)
    for h in range(B_HEADS):
        lanes = slice((h // 2) * 128, (h // 2) * 128 + 128)
        off = (h % 2) * hd
        to_low = _one_hot((128, 128), lambda r, c: (c < hd) & (r == c + off))
        place_c = _one_hot((384, 128), lambda r, c: (r % 128 == h) & (c == hd + r // 128))
        v_rows = _one_hot((v_rows_n, 128), lambda r, c: (r < hd) & (c == r + off))
        q = (x[:, lanes] * (scale * LOG2_E)).astype(BF16)
        k = x[:, width:2 * width][:, lanes].astype(BF16)
        v = x[:, 2 * width:3 * width][:, lanes].astype(BF16)
        qa_ref[h] = jnp.where(ones_lanes, 1.0, _dot(q, to_low)).astype(BF16)
        ka_ref[h] = (_dot(k, to_low) + _dot(c_split, place_c)).astype(BF16)
        vt_ref[h, 0] = jnp.where(row_t == hd, 1.0, _dot_nt(v_rows, v)).astype(BF16)


def fox_prep(x, g_norm, w_b, f_bias_row, seq, tm):
    n, d = x.shape
    return pl.pallas_call(
        functools.partial(_fox_prep_kernel, seq=seq, scale=B_HEAD_DIM ** -0.5),
        grid=(n // tm,),
        in_specs=[pl.BlockSpec((tm, d), lambda i: (i, 0)),
                  _const_spec((1, d)), _const_spec(w_b.shape), _const_spec((1, 128))],
        out_specs=[pl.BlockSpec((B_HEADS, tm, 128), lambda i: (0, i, 0)),
                   pl.BlockSpec((B_HEADS, tm, 128), lambda i: (0, i, 0)),
                   pl.BlockSpec((B_HEADS, 1, FOX_V_ROWS, tm), lambda i: (0, i, 0, 0))],
        out_shape=[jax.ShapeDtypeStruct((B_HEADS, n, 128), BF16),
                   jax.ShapeDtypeStruct((B_HEADS, n, 128), BF16),
                   jax.ShapeDtypeStruct((B_HEADS, n // tm, FOX_V_ROWS, tm), BF16)],
        scratch_shapes=[pltpu.VMEM((1, 128), F32)],
        compiler_params=_params("arbitrary"),
        name="fox_prep",
    )(x, g_norm, w_b, f_bias_row)


def _fox_kernel(qa_ref, ka_ref, vt_ref, o_ref, m_scr, acc_scr, s_scr):
    i = pl.program_id(2)
    heads, tq, _ = qa_ref.shape
    tk = vt_ref.shape[3]
    hd = B_HEAD_DIM
    m_scr[...] = jnp.full(m_scr.shape, -jnp.inf, F32)
    acc_scr[...] = jnp.zeros(acc_scr.shape, F32)

    hs = range(heads)

    def logits(j, slot):
        start = pl.multiple_of(j * tk, tk)
        for h in hs:
            s_scr[slot, h] = _dot_nt(ka_ref[h, pl.ds(start, tk), :], qa_ref[h])

    def consume(slot, j, masked):
        s = [s_scr[slot, h] for h in hs]
        if masked:
            key = lax.broadcasted_iota(jnp.int32, (tk, tq), 0)
            qry = lax.broadcasted_iota(jnp.int32, (tk, tq), 1)
            s = [jnp.where(key <= qry, x, -jnp.inf) for x in s]
        m_prev = [m_scr[h] for h in hs]
        m_new = [jnp.maximum(m_prev[h], jnp.max(s[h], axis=0, keepdims=True)) for h in hs]
        p = [jnp.exp2((s[h] - m_new[h]).astype(BF16)) for h in hs]
        pv = [_dot(vt_ref[h, j], p[h]) for h in hs]
        for h in hs:
            acc_scr[h] = jnp.exp2(m_prev[h] - m_new[h]) * acc_scr[h] + pv[h]
            m_scr[h] = m_new[h]

    logits(0, 0)

    def body(jj, carry):
        a = 2 * jj
        logits(a + 1, 1)
        consume(0, a, False)
        logits(a + 2, 0)
        consume(1, a + 1, False)
        return carry

    lax.fori_loop(0, i // 2, body, 0)

    @pl.when(i % 2 == 0)
    def _():
        consume(0, i, True)

    @pl.when(i % 2 == 1)
    def _():
        logits(i, 1)
        consume(0, i - 1, False)
        consume(1, i, True)

    out_t = [acc_scr[h, 0:hd, :] / acc_scr[h, hd:hd + 1, :] for h in range(heads)]
    o_ref[...] = jnp.concatenate(out_t, axis=0).T


def fox_attention(qa, ka, vt, bn, seq, tq, group):
    heads, n, _ = qa.shape
    rows = vt.shape[2]
    nq = seq // tq
    return pl.pallas_call(
        _fox_kernel,
        grid=(bn, heads // group, nq),
        in_specs=[pl.BlockSpec((group, tq, 128), lambda bi, hi, i: (hi, bi * nq + i, 0)),
                  pl.BlockSpec((group, seq, 128), lambda bi, hi, i: (hi, bi, 0)),
                  pl.BlockSpec((group, nq, rows, tq), lambda bi, hi, i: (hi, bi, 0, 0))],
        out_specs=pl.BlockSpec((tq, group * B_HEAD_DIM), lambda bi, hi, i: (bi * nq + i, hi)),
        out_shape=jax.ShapeDtypeStruct((n, heads * B_HEAD_DIM), F32),
        scratch_shapes=[pltpu.VMEM((group, 1, tq), F32), pltpu.VMEM((group, rows, tq), F32),
                        pltpu.VMEM((2, group, tq, tq), F32)],
        compiler_params=_params("parallel", "parallel", "arbitrary"),
        name="fox_attention",
    )(qa, ka, vt)


def _rwkv_prep_kernel(*refs, seq, mix_value):
    (x_ref, gn_ref, w_ref, mu_ref, w0_ref, wup_ref, a0_ref, aup_ref, gup_ref, kk_ref, ka_ref,
     rk_ref, hsum_ref) = refs[:13]
    if mix_value:
        vfirst_ref, v0_ref, vdown_ref, vup_ref = refs[13:17]
    r_out, ld_out, k_out, v_out, kk_out, b_out, g_out, bonus_out, last_scr = refs[-9:]
    tm = x_ref.shape[0]
    cw = r_out.shape[1]
    pc = _project(x_ref, gn_ref, w_ref)
    @pl.when((pl.program_id(0) * tm) % seq == 0)
    def _():
        last_scr[...] = jnp.zeros(last_scr.shape, F32)

    prev_row = last_scr[...]
    last_scr[...] = pc[tm - 1:tm, :]
    row = lax.broadcasted_iota(jnp.int32, pc.shape, 0)
    shifted = jnp.where(row == 0, prev_row, pltpu.roll(pc, 1, axis=0))
    pc = pc + (shifted - pc) * mu_ref[...]

    r = pc[:, 0:cw]
    k = pc[:, cw:2 * cw]
    v = pc[:, 2 * cw:3 * cw]
    lora = pc[:, 3 * cw:3 * cw + C_DECAY_LORA + C_AAA_LORA]
    g_lo = pc[:, 3 * cw + C_DECAY_LORA + C_AAA_LORA:]
    if mix_value:
        lam = jax.nn.sigmoid(v0_ref[...] + _mm(_mm(v, vdown_ref[...]), vup_ref[...]))
        v = v + (vfirst_ref[...] - v) * lam
    w = -jax.nn.softplus(-(w0_ref[...] + _dot_3pass(jnp.tanh(lora), wup_ref[...]))) - 0.5
    log_decay = -jnp.exp(w)
    a = jax.nn.sigmoid(a0_ref[...] + _mm(lora, aup_ref[...]))
    g = _mm(jax.nn.sigmoid(g_lo), gup_ref[...])
    kk = k * kk_ref[...]
    ss = _dot_exact_rhs(kk * kk, hsum_ref[...], 2)
    kk = kk / jnp.maximum(jnp.sqrt(ss), 1e-12)
    k = k * (1.0 + (a - 1.0) * ka_ref[...])
    r_out[...] = r
    ld_out[...] = log_decay
    k_out[...] = k
    v_out[...] = v
    kk_out[...] = kk
    b_out[...] = kk * a
    g_out[...] = g
    bonus_out[...] = _dot_exact_rhs(r * k * rk_ref[...], hsum_ref[...], 2) * v


def rwkv_prep(x, g_norm, w_c, mu, w0, w_up_pad, a0, a_up_pad, g_up, k_k, k_a, r_k, head_sum, seq, tm,
              v_first=None, v0=None, v_down=None, v_up=None):
    n, d = x.shape
    ccols = w_c.shape[1]
    cw = w0.shape[1]
    mix_value = v_first is not None
    const = _const_spec
    in_specs = [pl.BlockSpec((tm, d), lambda i: (i, 0)), const((1, d)), const(w_c.shape),
                const((1, ccols)), const((1, cw)), const(w_up_pad.shape), const((1, cw)),
                const(a_up_pad.shape), const(g_up.shape), const((1, cw)), const((1, cw)),
                const((1, cw)), const(head_sum.shape)]
    args = [x, g_norm, w_c, mu, w0, w_up_pad, a0, a_up_pad, g_up, k_k, k_a, r_k, head_sum]
    if mix_value:
        in_specs += [pl.BlockSpec((tm, cw), lambda i: (i, 0)), const((1, cw)),
                     const(v_down.shape), const(v_up.shape)]
        args += [v_first, v0, v_down, v_up]
    out = jax.ShapeDtypeStruct((n, cw), F32)
    return pl.pallas_call(
        functools.partial(_rwkv_prep_kernel, seq=seq, mix_value=mix_value),
        grid=(n // tm,),
        in_specs=in_specs,
        out_specs=[pl.BlockSpec((tm, cw), lambda i: (i, 0))] * 8,
        out_shape=[out] * 8,
        scratch_shapes=[pltpu.VMEM((1, ccols), F32)],
        compiler_params=_params("arbitrary"),
        name="rwkv_prep",
    )(*args)


def _lane_head(shape):
    return lax.broadcasted_iota(jnp.int32, shape, 1) // C_HEAD_DIM


def _stack_masked(x, pack):
    head = _lane_head(x.shape)
    return jnp.concatenate([jnp.where(head == h, x, 0.0) for h in range(pack)], axis=0)


def _diag_blocks(full, pack):
    n = C_HEAD_DIM
    head = _lane_head((n, pack * n))
    out = full[0:n]
    for h in range(1, pack):
        out = jnp.where(head == h, full[h * n:(h + 1) * n], out)
    return out


def _mm(a, b):
    return _dot(a.astype(BF16), b.astype(BF16))


def _mm_nt(a, b):
    return _dot_nt(a.astype(BF16), b.astype(BF16))


def _mm_tn(a, b):
    return _dot_tn(a.astype(BF16), b.astype(BF16))


def _rwkv_chunk_kernel(r_ref, ld_ref, k_ref, v_ref, kk_ref, b_ref,
                       rq_out, y0_out, g_out, h_out, *, pack):
    tc, cw = r_ref.shape
    c = RWKV_CHUNK
    pw = pack * C_HEAD_DIM
    row = lax.broadcasted_iota(jnp.int32, (c, pw), 0)
    col = lax.broadcasted_iota(jnp.int32, (c, pw), 1) % C_HEAD_DIM
    incl = col <= row
    strict = col < row
    eye = (col == row).astype(F32)
    tril_ones = (lax.broadcasted_iota(jnp.int32, (c, c), 1)
                 <= lax.broadcasted_iota(jnp.int32, (c, c), 0)).astype(BF16)

    units = [(slice(ci * c, (ci + 1) * c), slice(p * pw, (p + 1) * pw))
             for ci in range(tc // c) for p in range(cw // pw)]
    each = lambda fn, *lists: [fn(*args) for args in zip(*lists)]
    stack = lambda x: _stack_masked(x, pack)
    diag = lambda x: _diag_blocks(x, pack)

    cums = {}
    for rows, _ in units:
        if rows.start not in cums:
            cums[rows.start] = _dot_exact_lhs(tril_ones, ld_ref[rows, :], 3)
    cum = [cums[rows.start][:, lanes] for rows, lanes in units]
    ld = [ld_ref[u] for u in units]
    v = [v_ref[u] for u in units]
    p_incl = each(jnp.exp, cum)
    p_inv = each(lambda x: jnp.exp(-x), cum)
    p_prev = each(lambda x, y: jnp.exp(x - y), cum, ld)
    r_t = each(lambda u, s: r_ref[u] * s, units, p_incl)
    kk_t = each(lambda u, s: kk_ref[u] * s, units, p_prev)
    b_t = each(lambda u, s: b_ref[u] * s, units, p_inv)
    k_t = each(lambda u, s: k_ref[u] * s, units, p_inv)

    lhs = each(lambda x, y: jnp.concatenate([x, y], axis=0).astype(BF16), kk_t, r_t)
    with_b = each(lambda x, y: _dot_nt(x, stack(y).astype(BF16)), lhs, b_t)
    with_k = each(lambda x, y: _dot_nt(x, stack(y).astype(BF16)), lhs, k_t)
    a_b = each(lambda x: jnp.where(strict, x[:c], 0.0), with_b)
    a_rb = each(lambda x: jnp.where(incl, x[c:], 0.0).astype(BF16), with_b)
    a_kk = each(lambda x: jnp.concatenate([jnp.where(strict, x[:c], 0.0),
                                           jnp.where(incl, x[c:], 0.0)], axis=0), with_k)
    av = each(lambda x, y: _mm(x, stack(y)), a_kk, v)

    lower_left = lambda m: (((row // (2 * m)) == (col // (2 * m)))
                            & ((row // m) % 2 == 1) & ((col // m) % 2 == 0))
    t_inv = each(lambda x: eye - jnp.where(lower_left(1), x, 0.0), a_b)
    m = 2
    while m < c:
        mask = lower_left(m)
        xl = each(lambda x, y: _mm(x, stack(jnp.where(mask, y, 0.0))), t_inv, a_b)
        t_inv = each(lambda x, y: x - _mm(y, stack(x)), t_inv, xl)
        m *= 2

    t_bf = each(lambda x: x.astype(BF16), t_inv)
    w = each(lambda x, y: _dot(x, stack(y).astype(BF16)), t_bf, kk_t)
    u0 = each(lambda x, y: _dot(x, stack(y[:c]).astype(BF16)), t_bf, av)
    for i, u in enumerate(units):
        rq_out[u] = r_t[i] - _dot(a_rb[i], stack(w[i]).astype(BF16))
        y0_out[u] = av[i][c:] - _dot(a_rb[i], stack(u0[i]).astype(BF16))
    for i, u in enumerate(units):
        p_end = p_incl[i][c - 1:c, :]
        g_out[u] = (eye - diag(_mm_tn(w[i], b_t[i]))) * p_end
        h_out[u] = (diag(_mm_tn(v[i], k_t[i])) - diag(_mm_tn(u0[i], b_t[i]))) * p_end


def rwkv_chunk(r, ld, k, v, kk, b, tc):
    n, cw = r.shape
    spec = pl.BlockSpec((tc, cw), lambda i: (i, 0))
    out = jax.ShapeDtypeStruct((n, cw), F32)
    return pl.pallas_call(
        functools.partial(_rwkv_chunk_kernel, pack=RWKV_PACK),
        grid=(n // tc,),
        in_specs=[spec] * 6,
        out_specs=[spec] * 4,
        out_shape=[out] * 4,
        compiler_params=_params("parallel"),
        name="rwkv_chunk",
    )(r, ld, k, v, kk, b)


def _rwkv_state_kernel(rq_ref, y0_ref, g_ref, h_ref, y_out, state_scr, *, pack):
    bn, tc, cw = rq_ref.shape
    c = RWKV_CHUNK
    pw = pack * C_HEAD_DIM
    groups = cw // pw

    @pl.when(pl.program_id(0) == 0)
    def _():
        state_scr[...] = jnp.zeros(state_scr.shape, F32)

    chains = [(bi, p, slice(p * pw, (p + 1) * pw)) for bi in range(bn) for p in range(groups)]
    state = [state_scr[bi, p] for bi, p, _ in chains]
    for ci in range(tc // c):
        rows = slice(ci * c, (ci + 1) * c)
        for i, (bi, p, lanes) in enumerate(chains):
            y_out[bi, rows, lanes] = _mm_nt(rq_ref[bi, rows, lanes], state[i]) + y0_ref[bi, rows, lanes]
        for i, (bi, p, lanes) in enumerate(chains):
            g_bd = _stack_masked(g_ref[bi, rows, lanes], pack)
            h_bd = _stack_masked(h_ref[bi, rows, lanes], pack)
            state[i] = _mm(state[i], g_bd) + h_bd
    for i, (bi, p, _) in enumerate(chains):
        state_scr[bi, p] = state[i]


def rwkv_state(rq, y0, g, h, tc):
    bn, s, cw = rq.shape
    pw = RWKV_PACK * C_HEAD_DIM
    spec = pl.BlockSpec((bn, tc, cw), lambda t: (0, t, 0))
    return pl.pallas_call(
        functools.partial(_rwkv_state_kernel, pack=RWKV_PACK),
        grid=(s // tc,),
        in_specs=[spec] * 4,
        out_specs=spec,
        out_shape=jax.ShapeDtypeStruct(rq.shape, F32),
        scratch_shapes=[pltpu.VMEM((bn, cw // pw, pw, pw), F32)],
        compiler_params=_params("arbitrary"),
        name="rwkv_state",
    )(rq, y0, g, h)


def _merge_kernel(x_ref, gn_ref, wg_ref, ya_ref, yb_ref, yc_ref, bonus_ref, g_ref, gb_ref, lng_ref, lnb_ref,
                  hsum_ref, pa_ref, pb_ref, pc_ref, wo_ref, o_ref):
    d = x_ref.shape[1]
    x = x_ref[...]
    h = _rmsnorm(x, gn_ref[...]).astype(BF16)
    y = yc_ref[...]
    inv_n = 1.0 / C_HEAD_DIM
    mu = _dot_exact_rhs(y, hsum_ref[...], 2) * inv_n
    yc = y - mu
    var = _dot_exact_rhs(yc * yc, hsum_ref[...], 2) * inv_n
    yc = yc * lax.rsqrt(var + GN_EPS) * lng_ref[...] + lnb_ref[...]
    yc = (yc + bonus_ref[...]) * g_ref[...]
    branches = ((ya_ref, pa_ref), (yb_ref, pb_ref), (yc, pc_ref))
    merged = None
    for bi, (y_b, p_ref) in enumerate(branches):
        y_b = y_b if bi == 2 else y_b[...]
        gate = jax.nn.sigmoid(_dot(h, wg_ref[:, bi * d:(bi + 1) * d]) + gb_ref[bi:bi + 1, :])
        term = gate * _dot(y_b.astype(BF16), p_ref[...])
        merged = term if merged is None else merged + term
    o_ref[...] = x + _dot(merged.astype(BF16), wo_ref[...])


def merge_out(x, g_norm, w_g, ya, yb, yc, bonus, g, gate_bias, lnx_g, lnx_b, head_sum,
              p_a, p_b, p_c, w_out, tm):
    n, d = x.shape
    tile = lambda a: pl.BlockSpec((tm, a.shape[1]), lambda i: (i, 0))
    const = lambda a: _const_spec(a.shape)
    return pl.pallas_call(
        _merge_kernel,
        grid=(n // tm,),
        in_specs=[tile(x), const(g_norm), const(w_g), tile(ya), tile(yb), tile(yc), tile(bonus), tile(g),
                  const(gate_bias), const(lnx_g), const(lnx_b), const(head_sum),
                  const(p_a), const(p_b), const(p_c), const(w_out)],
        out_specs=pl.BlockSpec((tm, d), lambda i: (i, 0)),
        out_shape=jax.ShapeDtypeStruct((n, d), F32),
        compiler_params=_params("parallel"),
        name="merge_out",
    )(x, g_norm, w_g, ya, yb, yc, bonus, g, gate_bias, lnx_g, lnx_b, head_sum, p_a, p_b, p_c, w_out)


def _ffn_kernel(x_ref, g_ref, wg_ref, wu_ref, wd_ref, gf_ref, o_ref, h_scr, acc_scr, *, final_norm):
    j = pl.program_id(1)

    @pl.when(j == 0)
    def _():
        x = x_ref[...]
        h_scr[...] = _rmsnorm(x, g_ref[...]).astype(BF16)
        acc_scr[...] = x

    h = h_scr[...]
    act = jax.nn.silu(_dot(h, wg_ref[...])) * _dot(h, wu_ref[...])
    acc_scr[...] += _dot(act.astype(BF16), wd_ref[...])

    @pl.when(j == pl.num_programs(1) - 1)
    def _():
        y = acc_scr[...]
        o_ref[...] = _rmsnorm(y, gf_ref[...]) if final_norm else y


def ffn(x, g, w_gate_up, w_down, g_final, tm, tf, final_norm):
    n, d = x.shape
    dff = w_down.shape[0]
    nf = dff // tf
    return pl.pallas_call(
        functools.partial(_ffn_kernel, final_norm=final_norm),
        grid=(n // tm, nf),
        in_specs=[pl.BlockSpec((tm, d), lambda i, j: (i, 0)),
                  pl.BlockSpec((1, d), lambda i, j: (0, 0)),
                  pl.BlockSpec((d, tf), lambda i, j: (0, j)),
                  pl.BlockSpec((d, tf), lambda i, j: (0, j + nf)),
                  pl.BlockSpec((tf, d), lambda i, j: (j, 0)),
                  pl.BlockSpec((1, d), lambda i, j: (0, 0))],
        out_specs=pl.BlockSpec((tm, d), lambda i, j: (i, 0)),
        out_shape=jax.ShapeDtypeStruct((n, d), F32),
        scratch_shapes=[pltpu.VMEM((tm, d), BF16), pltpu.VMEM((tm, d), F32)],
        compiler_params=_params("parallel", "arbitrary"),
        name="ffn",
    )(x, g, w_gate_up, w_gate_up, w_down, g_final)


def _pad_cols(w, mult):
    pad = (-w.shape[1]) % mult
    return jnp.pad(w, ((0, 0), (0, pad))) if pad else w


def _largest_tile(total, cap):
    t = cap
    while total % t:
        t //= 2
    return t


def kernel(x, norm_mix, w_in, gate_bias, a_ln_g, a_ln_b, a_w_s, a_b_s, b_f_bias, c_mu, c_w0, c_w_up, c_a0, c_a_up, c_g_up, c_k_k, c_k_a, c_r_k, c_lnx_g, c_lnx_b, c_v0, c_v_down, c_v_up, p_a, p_b, p_c, w_out, norm_ffn, w_gate_up, w_down, norm_final):
    bn, s, d = x.shape
    depth = w_in.shape[0]
    n = bn * s
    a_width = a_ln_g.shape[1]
    b_width = B_HEADS * B_HEAD_DIM
    c_width = C_HEADS * C_HEAD_DIM
    a_cols = 2 * a_width
    b_cols = 3 * b_width + B_HEADS
    c_cols = 3 * c_width + C_DECAY_LORA + C_AAA_LORA + C_GATE_LORA
    dff = w_down.shape[1]
    assert s % 128 == 0 and dff % 256 == 0

    tm = _largest_tile(n, 512)
    t_attn = _largest_tile(s, 512)
    t_chunk = _largest_tile(s, 256)
    t_state = _largest_tile(s, 256)
    tf = dff // 2 if (dff // 2) % 128 == 0 else dff

    head_sum = jnp.kron(jnp.eye(C_HEADS, dtype=BF16), jnp.ones((C_HEAD_DIM, C_HEAD_DIM), BF16))
    row2 = lambda p: p.reshape(1, -1)

    xf = x.reshape(n, d)
    v_first = None
    for l in range(depth):
        w_l = w_in[l].astype(BF16)
        w_a = w_l[:, :a_cols]
        w_b = _pad_cols(w_l[:, a_cols:a_cols + b_cols], 128)
        w_c = w_l[:, a_cols + b_cols:a_cols + b_cols + c_cols]
        w_g = w_l[:, a_cols + b_cols + c_cols:]
        g_mix = row2(norm_mix[l])

        b_s_full = jnp.repeat(a_b_s[l].T, a_width // A_GROUPS, axis=1)
        ya = gmlp(xf, g_mix, w_a, row2(a_ln_g[l]), row2(a_ln_b[l]), a_w_s[l], b_s_full, tm)

        f_bias_row = jnp.pad(b_f_bias[l], (0, 128 - B_HEADS)).reshape(1, 128)
        qa, ka, vt = fox_prep(xf, g_mix, w_b, f_bias_row, s, tm)
        yb = fox_attention(qa, ka, vt, bn, s, tm, FOX_HEAD_GROUP)

        zeros_lora = jnp.zeros((C_AAA_LORA, c_width), F32)
        w_up_pad = jnp.concatenate([c_w_up[l], zeros_lora], axis=0)
        a_up_pad = jnp.concatenate([jnp.zeros((C_DECAY_LORA, c_width), F32), c_a_up[l]], axis=0)
        mix = {} if l == 0 else dict(v_first=v_first, v0=row2(c_v0[l - 1]),
                                     v_down=c_v_down[l - 1], v_up=c_v_up[l - 1])
        r_c, ld_c, k_c, v_c, kk_c, b_c, g_c, bonus_c = rwkv_prep(
            xf, g_mix, w_c, row2(c_mu[l]), row2(c_w0[l]), w_up_pad, row2(c_a0[l]), a_up_pad, c_g_up[l],
            row2(c_k_k[l]), row2(c_k_a[l]), row2(c_r_k[l]), head_sum, s, tm, **mix)
        if l == 0:
            v_first = v_c
        rq, y0, g_mat, h_mat = rwkv_chunk(r_c, ld_c, k_c, v_c, kk_c, b_c, t_chunk)
        per_batch = lambda t: t.reshape(bn, s, c_width)
        yc = rwkv_state(per_batch(rq), per_batch(y0), per_batch(g_mat), per_batch(h_mat), t_state)
        yc = yc.reshape(n, c_width)

        xf = merge_out(xf, g_mix, w_g, ya, yb, yc, bonus_c, g_c, gate_bias[l], row2(c_lnx_g[l]),
                       row2(c_lnx_b[l]), head_sum, p_a[l].astype(BF16), p_b[l].astype(BF16),
                       p_c[l].astype(BF16), w_out[l].astype(BF16), tm)
        xf = ffn(xf, row2(norm_ffn[l]), w_gate_up[l].astype(BF16), w_down[l].astype(BF16),
                 row2(norm_final), tm, tf, final_norm=(l == depth - 1))
    return xf.reshape(bn, s, d)
```

```python
import functools

import jax
import jax.numpy as jnp
from jax import lax
from jax.experimental import pallas as pl
from jax.experimental.pallas import tpu as pltpu

F32 = jnp.float32
BF16 = jnp.bfloat16

V7X_VMEM_LIMIT_BYTES = 56 * 1024 * 1024

NORM_EPS = 1e-6
LN_EPS = 1e-5
GN_EPS = 64e-5

N_BRANCH = 3
A_GROUPS = 4
A_CHUNK = 128
B_HEADS = 8
B_HEAD_DIM = 64
C_HEADS = 8
C_HEAD_DIM = 64
C_DECAY_LORA = 64
C_AAA_LORA = 64
C_GATE_LORA = 128
PROJECT_ROWS = 256
LOG2_E = 1.4426950408889634
DECAY_SCALE = 0.6065306597126334
FOX_HEAD_GROUP = 4
FOX_V_ROWS = 80
RWKV_CHUNK = 64
RWKV_PACK = 2


def _params(*semantics):
    return pltpu.CompilerParams(dimension_semantics=semantics,
                                vmem_limit_bytes=V7X_VMEM_LIMIT_BYTES)


def _rmsnorm(x, g):
    return x * lax.rsqrt(jnp.mean(x * x, axis=-1, keepdims=True) + NORM_EPS) * g


def _dot(a, b, precision=None):
    return jnp.dot(a, b, preferred_element_type=F32, precision=precision)


def _dot_nt(a, b, precision=None):
    return lax.dot_general(a, b, (((1,), (1,)), ((), ())),
                           preferred_element_type=F32, precision=precision)


def _bf16_terms(x, n):
    terms = []
    for _ in range(n - 1):
        t = x.astype(BF16)
        terms.append(t)
        x = x - t.astype(F32)
    terms.append(x.astype(BF16))
    return terms


def _dot_exact_rhs(x, m, n_terms):
    return sum(_dot(t, m) for t in _bf16_terms(x, n_terms))


def _dot_exact_lhs(m, x, n_terms):
    return sum(_dot(m, t) for t in _bf16_terms(x, n_terms))


def _dot_3pass(a, b):
    a_hi, a_lo = _bf16_terms(a, 2)
    b_hi, b_lo = _bf16_terms(b, 2)
    return _dot(a_hi, b_hi) + (_dot(a_lo, b_hi) + _dot(a_hi, b_lo))


def _dot_tn(a, b, precision=None):
    return lax.dot_general(a, b, (((0,), (0,)), ((), ())),
                           preferred_element_type=F32, precision=precision)


def _const_spec(shape):
    return pl.BlockSpec(shape, lambda *_: (0,) * len(shape), pipeline_mode=pl.Buffered(1))


def _project(x_ref, g_ref, w_ref, rows=slice(None)):
    return _dot(_rmsnorm(x_ref[rows, :], g_ref[...]).astype(BF16), w_ref[...])


def _interleaved(n_parts, produce, consume):
    ready = produce(0)
    for k in range(n_parts):
        upcoming = produce(k + 1) if k + 1 < n_parts else None
        consume(k, ready)
        ready = upcoming


def _gmlp_kernel(x_ref, gn_ref, w_ref, lng_ref, lnb_ref, ws_ref, bs_ref, o_ref, *, width):
    tm = x_ref.shape[0]
    gd = width // A_GROUPS
    part = min(tm, PROJECT_ROWS)
    row = lax.broadcasted_iota(jnp.int32, (A_CHUNK, A_CHUNK), 0)
    col = lax.broadcasted_iota(jnp.int32, (A_CHUNK, A_CHUNK), 1)
    causal = col <= row

    def mix(k, pa):
        for c in range(part // A_CHUNK):
            rows = slice(c * A_CHUNK, (c + 1) * A_CHUNK)
            out_rows = slice(k * part + c * A_CHUNK, k * part + (c + 1) * A_CHUNK)
            act = jax.nn.gelu(pa[rows, :])
            u = act[:, :width]
            v = act[:, width:]
            mu = jnp.mean(v, axis=-1, keepdims=True)
            var = jnp.mean(jnp.square(v - mu), axis=-1, keepdims=True)
            v = (v - mu) * lax.rsqrt(var + LN_EPS) * lng_ref[...] + lnb_ref[...]
            v = v.astype(BF16)
            for g in range(A_GROUPS):
                cols = slice(g * gd, (g + 1) * gd)
                w = jnp.where(causal, ws_ref[g], 0.0).astype(BF16)
                mixed = _dot(w, v[:, cols]) + bs_ref[:, cols]
                o_ref[out_rows, cols] = u[:, cols] * mixed

    _interleaved(tm // part,
                 lambda k: _project(x_ref, gn_ref, w_ref, slice(k * part, (k + 1) * part)), mix)


def gmlp(x, g_norm, w_a, ln_g, ln_b, w_s, b_s_full, tm):
    n, d = x.shape
    width = w_a.shape[1] // 2
    return pl.pallas_call(
        functools.partial(_gmlp_kernel, width=width),
        grid=(n // tm,),
        in_specs=[pl.BlockSpec((tm, d), lambda i: (i, 0)),
                  _const_spec((1, d)), _const_spec(w_a.shape),
                  _const_spec((1, width)), _const_spec((1, width)),
                  _const_spec((A_GROUPS, A_CHUNK, A_CHUNK)), _const_spec((A_CHUNK, width))],
        out_specs=pl.BlockSpec((tm, width), lambda i: (i, 0)),
        out_shape=jax.ShapeDtypeStruct((n, width), F32),
        compiler_params=_params("parallel"),
        name="gmlp",
    )(x, g_norm, w_a, ln_g, ln_b, w_s, b_s_full)


def _one_hot(shape, hit):
    r = lax.broadcasted_iota(jnp.int32, shape, 0)
    c = lax.broadcasted_iota(jnp.int32, shape, 1)
    return jnp.where(hit(r, c), 1.0, 0.0).astype(BF16)


def _fox_prep_kernel(x_ref, gn_ref, w_ref, fb_ref, qa_ref, ka_ref, vt_ref, carry_scr, *, seq, scale):
    tm = x_ref.shape[0]
    hd = B_HEAD_DIM
    width = B_HEADS * hd
    x = _project(x_ref, gn_ref, w_ref)

    @pl.when((pl.program_id(0) * tm) % seq == 0)
    def _():
        carry_scr[...] = jnp.zeros(carry_scr.shape, F32)

    log_f = jax.nn.log_sigmoid(x[:, 3 * width:3 * width + 128] + fb_ref[...]) * LOG2_E
    tril = (lax.broadcasted_iota(jnp.int32, (tm, tm), 1)
            <= lax.broadcasted_iota(jnp.int32, (tm, tm), 0)).astype(BF16)
    cum = _dot_exact_lhs(tril, log_f, 3) + carry_scr[...]
    carry_scr[...] = cum[tm - 1:tm, :]
    lane = lax.broadcasted_iota(jnp.int32, (tm, 128), 1)
    terms = _bf16_terms(jnp.where(lane < B_HEADS, -cum, 0.0), 3)
    c_terms = sum(pltpu.roll(t.astype(F32), B_HEADS * i, axis=1) if i else t.astype(F32)
                  for i, t in enumerate(terms)).astype(BF16)
    place_c = _one_hot((128, B_HEADS * 128),
                       lambda r, c: (r < 3 * B_HEADS) & (c == 128 * (r % B_HEADS) + hd + r // B_HEADS))
    c_placed = _dot(c_terms, place_c)

    low = lane < hd
    ones_lanes = (lane >= hd) & (lane < hd + 3)
    v_rows_n = vt_ref.shape[2]
    row_t = lax.broadcasted_iota(jnp.int32, (v_rows_n, tm), 0)
    for pair in range(B_HEADS // 2):
        lanes = slice(pair * 128, pair * 128 + 128)
        q = x[:, lanes] * (scale * LOG2_E)
        k = x[:, width:2 * width][:, lanes]
        v = x[:, 2 * width:3 * width][:, lanes].astype(BF16)
        for odd in range(2):
            h = 2 * pair + odd
            q_h, k_h = (pltpu.roll(q, hd, axis=1), pltpu.roll(k, hd, axis=1)) if odd else (q, k)
            v_rows = _one_hot((v_rows_n, 128), lambda r, c: (r < hd) & (c == r + odd * hd))
            qa_ref[h] = jnp.where(ones_lanes, 1.0, jnp.where(low, q_h, 0.0)).astype(BF16)
            ka_ref[h] = (jnp.where(low, k_h, 0.0) + c_placed[:, h * 128:(h + 1) * 128]).astype(BF16)
            vt_ref[h, 0] = jnp.where(row_t == hd, 1.0, _dot_nt(v_rows, v)).astype(BF16)


def fox_prep(x, g_norm, w_b, f_bias_row, seq, tm):
    n, d = x.shape
    return pl.pallas_call(
        functools.partial(_fox_prep_kernel, seq=seq, scale=B_HEAD_DIM ** -0.5),
        grid=(n // tm,),
        in_specs=[pl.BlockSpec((tm, d), lambda i: (i, 0)),
                  _const_spec((1, d)), _const_spec(w_b.shape), _const_spec((1, 128))],
        out_specs=[pl.BlockSpec((B_HEADS, tm, 128), lambda i: (0, i, 0)),
                   pl.BlockSpec((B_HEADS, tm, 128), lambda i: (0, i, 0)),
                   pl.BlockSpec((B_HEADS, 1, FOX_V_ROWS, tm), lambda i: (0, i, 0, 0))],
        out_shape=[jax.ShapeDtypeStruct((B_HEADS, n, 128), BF16),
                   jax.ShapeDtypeStruct((B_HEADS, n, 128), BF16),
                   jax.ShapeDtypeStruct((B_HEADS, n // tm, FOX_V_ROWS, tm), BF16)],
        scratch_shapes=[pltpu.VMEM((1, 128), F32)],
        compiler_params=_params("arbitrary"),
        name="fox_prep",
    )(x, g_norm, w_b, f_bias_row)


def _fox_kernel(qa_ref, ka_ref, vt_ref, o_ref, m_scr, acc_scr, s_scr, smax_scr):
    i = pl.program_id(2)
    heads, tq, _ = qa_ref.shape
    tk = vt_ref.shape[3]
    hd = B_HEAD_DIM
    m_scr[...] = jnp.full(m_scr.shape, -jnp.inf, F32)
    acc_scr[...] = jnp.zeros(acc_scr.shape, F32)

    hs = range(heads)

    def logits(j, slot, h):
        start = pl.multiple_of(j * tk, tk)
        s = _dot_nt(ka_ref[h, pl.ds(start, tk), :], qa_ref[h])
        s_scr[slot, h] = s
        smax_scr[slot, h] = jnp.max(s, axis=0, keepdims=True)

    def consume(slot, j, h, masked):
        s = s_scr[slot, h]
        if masked:
            key = lax.broadcasted_iota(jnp.int32, (tk, tq), 0)
            qry = lax.broadcasted_iota(jnp.int32, (tk, tq), 1)
            s = jnp.where(key <= qry, s, -jnp.inf)
            s_max = jnp.max(s, axis=0, keepdims=True)
        else:
            s_max = smax_scr[slot, h]
        m_prev = m_scr[h]
        m_new = jnp.maximum(m_prev, s_max)
        p = jnp.exp2((s - m_new).astype(BF16))
        acc_scr[h] = jnp.exp2(m_prev - m_new) * acc_scr[h] + _dot(vt_ref[h, j], p)
        m_scr[h] = m_new

    def advance(cur, cur_slot, masked=False, nxt=None):
        for h in hs:
            if nxt is not None:
                logits(nxt, 1 - cur_slot, h)
            consume(cur_slot, cur, h, masked)

    for h in hs:
        logits(0, 0, h)

    def body(jj, carry):
        a = 2 * jj
        advance(a, 0, nxt=a + 1)
        advance(a + 1, 1, nxt=a + 2)
        return carry

    lax.fori_loop(0, i // 2, body, 0)

    @pl.when(i % 2 == 0)
    def _():
        advance(i, 0, masked=True)

    @pl.when(i % 2 == 1)
    def _():
        advance(i - 1, 0, nxt=i)
        advance(i, 1, masked=True)

    out_t = [acc_scr[h, 0:hd, :] / acc_scr[h, hd:hd + 1, :] for h in range(heads)]
    o_ref[...] = jnp.concatenate(out_t, axis=0).T


def fox_attention(qa, ka, vt, bn, seq, tq, group):
    heads, n, _ = qa.shape
    rows = vt.shape[2]
    nq = seq // tq
    return pl.pallas_call(
        _fox_kernel,
        grid=(bn, heads // group, nq),
        in_specs=[pl.BlockSpec((group, tq, 128), lambda bi, hi, i: (hi, bi * nq + i, 0)),
                  pl.BlockSpec((group, seq, 128), lambda bi, hi, i: (hi, bi, 0)),
                  pl.BlockSpec((group, nq, rows, tq), lambda bi, hi, i: (hi, bi, 0, 0))],
        out_specs=pl.BlockSpec((tq, group * B_HEAD_DIM), lambda bi, hi, i: (bi * nq + i, hi)),
        out_shape=jax.ShapeDtypeStruct((n, heads * B_HEAD_DIM), F32),
        scratch_shapes=[pltpu.VMEM((group, 1, tq), F32), pltpu.VMEM((group, rows, tq), F32),
                        pltpu.VMEM((2, group, tq, tq), F32), pltpu.VMEM((2, group, 1, tq), F32)],
        compiler_params=_params("parallel", "parallel", "arbitrary"),
        name="fox_attention",
    )(qa, ka, vt)


def _rwkv_prep_kernel(*refs, seq, mix_value):
    (x_ref, gn_ref, w_ref, mu_ref, w0_ref, wup_ref, a0_ref, aup_ref, gup_ref, kk_ref, ka_ref,
     rk_ref, hsum_ref) = refs[:13]
    if mix_value:
        vfirst_ref, v0_ref, vdown_ref, vup_ref = refs[13:17]
    r_out, ld_out, k_out, v_out, kk_out, b_out, g_out, bonus_out, last_scr = refs[-9:]
    tm = x_ref.shape[0]
    cw = r_out.shape[1]
    part = min(tm, PROJECT_ROWS)

    @pl.when((pl.program_id(0) * tm) % seq == 0)
    def _():
        last_scr[...] = jnp.zeros(last_scr.shape, F32)

    def derive(part_idx, pc):
        rows = slice(part_idx * part, (part_idx + 1) * part)
        prev_row = last_scr[...]
        last_scr[...] = pc[part - 1:part, :]
        row = lax.broadcasted_iota(jnp.int32, pc.shape, 0)
        shifted = jnp.where(row == 0, prev_row, pltpu.roll(pc, 1, axis=0))
        pc = pc + (shifted - pc) * mu_ref[...]

        r = pc[:, 0:cw]
        k = pc[:, cw:2 * cw]
        v = pc[:, 2 * cw:3 * cw]
        lora = pc[:, 3 * cw:3 * cw + C_DECAY_LORA + C_AAA_LORA]
        g_lo = pc[:, 3 * cw + C_DECAY_LORA + C_AAA_LORA:]
        if mix_value:
            lam = jax.nn.sigmoid(v0_ref[...] + _mm(_mm(v, vdown_ref[...]), vup_ref[...]))
            v = v + (vfirst_ref[rows, :] - v) * lam
        z = w0_ref[...] + _dot_3pass(jnp.tanh(lora), wup_ref[...])
        log_decay = -DECAY_SCALE * jax.nn.sigmoid(z)
        a = jax.nn.sigmoid(a0_ref[...] + _mm(lora, aup_ref[...]))
        g = _mm(jax.nn.sigmoid(g_lo), gup_ref[...])
        kk = k * kk_ref[...]
        ss = _dot_exact_rhs(kk * kk, hsum_ref[...], 2)
        kk = kk / jnp.maximum(jnp.sqrt(ss), 1e-12)
        k = k * (1.0 + (a - 1.0) * ka_ref[...])
        r_out[rows, :] = r
        ld_out[rows, :] = log_decay
        k_out[rows, :] = k
        v_out[rows, :] = v
        kk_out[rows, :] = kk
        b_out[rows, :] = kk * a
        g_out[rows, :] = g
        bonus_out[rows, :] = _dot_exact_rhs(r * k * rk_ref[...], hsum_ref[...], 2) * v

    _interleaved(tm // part,
                 lambda k: _project(x_ref, gn_ref, w_ref, slice(k * part, (k + 1) * part)), derive)


def rwkv_prep(x, g_norm, w_c, mu, w0, w_up_pad, a0, a_up_pad, g_up, k_k, k_a, r_k, head_sum, seq, tm,
              v_first=None, v0=None, v_down=None, v_up=None):
    n, d = x.shape
    ccols = w_c.shape[1]
    cw = w0.shape[1]
    mix_value = v_first is not None
    const = _const_spec
    in_specs = [pl.BlockSpec((tm, d), lambda i: (i, 0)), const((1, d)), const(w_c.shape),
                const((1, ccols)), const((1, cw)), const(w_up_pad.shape), const((1, cw)),
                const(a_up_pad.shape), const(g_up.shape), const((1, cw)), const((1, cw)),
                const((1, cw)), const(head_sum.shape)]
    args = [x, g_norm, w_c, mu, w0, w_up_pad, a0, a_up_pad, g_up, k_k, k_a, r_k, head_sum]
    if mix_value:
        in_specs += [pl.BlockSpec((tm, cw), lambda i: (i, 0)), const((1, cw)),
                     const(v_down.shape), const(v_up.shape)]
        args += [v_first, v0, v_down, v_up]
    out = jax.ShapeDtypeStruct((n, cw), F32)
    return pl.pallas_call(
        functools.partial(_rwkv_prep_kernel, seq=seq, mix_value=mix_value),
        grid=(n // tm,),
        in_specs=in_specs,
        out_specs=[pl.BlockSpec((tm, cw), lambda i: (i, 0))] * 8,
        out_shape=[out] * 8,
        scratch_shapes=[pltpu.VMEM((1, ccols), F32)],
        compiler_params=_params("arbitrary"),
        name="rwkv_prep",
    )(*args)


def _lane_head(shape):
    return lax.broadcasted_iota(jnp.int32, shape, 1) // C_HEAD_DIM


def _stack_masked(x, pack):
    head = _lane_head(x.shape)
    return jnp.concatenate([jnp.where(head == h, x, 0.0) for h in range(pack)], axis=0)


def _diag_blocks(full, pack):
    n = C_HEAD_DIM
    head = _lane_head((n, pack * n))
    out = full[0:n]
    for h in range(1, pack):
        out = jnp.where(head == h, full[h * n:(h + 1) * n], out)
    return out


def _mm(a, b):
    return _dot(a.astype(BF16), b.astype(BF16))


def _mm_nt(a, b):
    return _dot_nt(a.astype(BF16), b.astype(BF16))


def _mm_tn(a, b):
    return _dot_tn(a.astype(BF16), b.astype(BF16))


def _rwkv_chunk_kernel(r_ref, ld_ref, k_ref, v_ref, kk_ref, b_ref,
                       rq_out, y0_out, g_out, h_out, *, pack):
    tc, cw = r_ref.shape
    c = RWKV_CHUNK
    pw = pack * C_HEAD_DIM
    row = lax.broadcasted_iota(jnp.int32, (c, pw), 0)
    col = lax.broadcasted_iota(jnp.int32, (c, pw), 1) % C_HEAD_DIM
    incl = col <= row
    strict = col < row
    eye = (col == row).astype(F32)
    tril_ones = (lax.broadcasted_iota(jnp.int32, (c, c), 1)
                 <= lax.broadcasted_iota(jnp.int32, (c, c), 0)).astype(BF16)

    units = [(slice(ci * c, (ci + 1) * c), slice(p * pw, (p + 1) * pw))
             for ci in range(tc // c) for p in range(cw // pw)]
    each = lambda fn, *lists: [fn(*args) for args in zip(*lists)]
    stack = lambda x: _stack_masked(x, pack)
    diag = lambda x: _diag_blocks(x, pack)

    cums = {}
    for rows, _ in units:
        if rows.start not in cums:
            cums[rows.start] = _dot_exact_lhs(tril_ones, ld_ref[rows, :], 3)
    cum = [cums[rows.start][:, lanes] for rows, lanes in units]
    ld = [ld_ref[u] for u in units]
    v = [v_ref[u] for u in units]
    p_incl = each(jnp.exp, cum)
    p_inv = each(lambda x: jnp.exp(-x), cum)
    p_prev = each(lambda x, y: jnp.exp(x - y), cum, ld)
    r_t = each(lambda u, s: r_ref[u] * s, units, p_incl)
    kk_t = each(lambda u, s: kk_ref[u] * s, units, p_prev)
    b_t = each(lambda u, s: b_ref[u] * s, units, p_inv)
    k_t = each(lambda u, s: k_ref[u] * s, units, p_inv)

    lhs = each(lambda x, y: jnp.concatenate([x, y], axis=0).astype(BF16), kk_t, r_t)
    with_b = each(lambda x, y: _dot_nt(x, stack(y).astype(BF16)), lhs, b_t)
    with_k = each(lambda x, y: _dot_nt(x, stack(y).astype(BF16)), lhs, k_t)
    a_b = each(lambda x: jnp.where(strict, x[:c], 0.0), with_b)
    a_rb = each(lambda x: jnp.where(incl, x[c:], 0.0).astype(BF16), with_b)
    a_kk = each(lambda x: jnp.concatenate([jnp.where(strict, x[:c], 0.0),
                                           jnp.where(incl, x[c:], 0.0)], axis=0), with_k)
    av = each(lambda x, y: _mm(x, stack(y)), a_kk, v)

    lower_left = lambda m: (((row // (2 * m)) == (col // (2 * m)))
                            & ((row // m) % 2 == 1) & ((col // m) % 2 == 0))
    t_inv = each(lambda x: eye - jnp.where(lower_left(1), x, 0.0), a_b)
    m = 2
    while m < c:
        mask = lower_left(m)
        xl = each(lambda x, y: _mm(x, stack(jnp.where(mask, y, 0.0))), t_inv, a_b)
        t_inv = each(lambda x, y: x - _mm(y, stack(x)), t_inv, xl)
        m *= 2

    t_bf = each(lambda x: x.astype(BF16), t_inv)
    w = each(lambda x, y: _dot(x, stack(y).astype(BF16)), t_bf, kk_t)
    u0 = each(lambda x, y: _dot(x, stack(y[:c]).astype(BF16)), t_bf, av)
    for i, u in enumerate(units):
        rq_out[u] = r_t[i] - _dot(a_rb[i], stack(w[i]).astype(BF16))
        y0_out[u] = av[i][c:] - _dot(a_rb[i], stack(u0[i]).astype(BF16))
    for i, u in enumerate(units):
        p_end = p_incl[i][c - 1:c, :]
        g_out[u] = (eye - diag(_mm_tn(w[i], b_t[i]))) * p_end
        h_out[u] = (diag(_mm_tn(v[i], k_t[i])) - diag(_mm_tn(u0[i], b_t[i]))) * p_end


def rwkv_chunk(r, ld, k, v, kk, b, tc):
    n, cw = r.shape
    spec = pl.BlockSpec((tc, cw), lambda i: (i, 0))
    out = jax.ShapeDtypeStruct((n, cw), F32)
    return pl.pallas_call(
        functools.partial(_rwkv_chunk_kernel, pack=RWKV_PACK),
        grid=(n // tc,),
        in_specs=[spec] * 6,
        out_specs=[spec] * 4,
        out_shape=[out] * 4,
        compiler_params=_params("parallel"),
        name="rwkv_chunk",
    )(r, ld, k, v, kk, b)


def _rwkv_state_kernel(rq_ref, y0_ref, g_ref, h_ref, y_out, state_scr, *, pack):
    bn, tc, cw = rq_ref.shape
    c = RWKV_CHUNK
    pw = pack * C_HEAD_DIM
    groups = cw // pw

    @pl.when(pl.program_id(0) == 0)
    def _():
        state_scr[...] = jnp.zeros(state_scr.shape, F32)

    chains = [(bi, p, slice(p * pw, (p + 1) * pw)) for bi in range(bn) for p in range(groups)]
    state = [state_scr[bi, p] for bi, p, _ in chains]
    for ci in range(tc // c):
        rows = slice(ci * c, (ci + 1) * c)
        for i, (bi, p, lanes) in enumerate(chains):
            y_out[bi, rows, lanes] = _mm_nt(rq_ref[bi, rows, lanes], state[i]) + y0_ref[bi, rows, lanes]
        for i, (bi, p, lanes) in enumerate(chains):
            g_bd = _stack_masked(g_ref[bi, rows, lanes], pack)
            h_bd = _stack_masked(h_ref[bi, rows, lanes], pack)
            state[i] = _mm(state[i], g_bd) + h_bd
    for i, (bi, p, _) in enumerate(chains):
        state_scr[bi, p] = state[i]


def rwkv_state(rq, y0, g, h, tc):
    bn, s, cw = rq.shape
    pw = RWKV_PACK * C_HEAD_DIM
    spec = pl.BlockSpec((bn, tc, cw), lambda t: (0, t, 0))
    return pl.pallas_call(
        functools.partial(_rwkv_state_kernel, pack=RWKV_PACK),
        grid=(s // tc,),
        in_specs=[spec] * 4,
        out_specs=spec,
        out_shape=jax.ShapeDtypeStruct(rq.shape, F32),
        scratch_shapes=[pltpu.VMEM((bn, cw // pw, pw, pw), F32)],
        compiler_params=_params("arbitrary"),
        name="rwkv_state",
    )(rq, y0, g, h)


def _merge_kernel(x_ref, gn_ref, wg_ref, ya_ref, yb_ref, yc_ref, bonus_ref, g_ref, gb_ref, lng_ref, lnb_ref,
                  hsum_ref, pa_ref, pb_ref, pc_ref, wo_ref, o_ref):
    d = x_ref.shape[1]
    x = x_ref[...]
    h = _rmsnorm(x, gn_ref[...]).astype(BF16)
    y = yc_ref[...]
    inv_n = 1.0 / C_HEAD_DIM
    mu = _dot_exact_rhs(y, hsum_ref[...], 2) * inv_n
    yc = y - mu
    var = _dot_exact_rhs(yc * yc, hsum_ref[...], 2) * inv_n
    yc = yc * lax.rsqrt(var + GN_EPS) * lng_ref[...] + lnb_ref[...]
    yc = (yc + bonus_ref[...]) * g_ref[...]
    branches = ((ya_ref, pa_ref), (yb_ref, pb_ref), (yc, pc_ref))
    merged = None
    for bi, (y_b, p_ref) in enumerate(branches):
        y_b = y_b if bi == 2 else y_b[...]
        gate = jax.nn.sigmoid(_dot(h, wg_ref[:, bi * d:(bi + 1) * d]) + gb_ref[bi:bi + 1, :])
        term = gate * _dot(y_b.astype(BF16), p_ref[...])
        merged = term if merged is None else merged + term
    o_ref[...] = x + _dot(merged.astype(BF16), wo_ref[...])


def merge_out(x, g_norm, w_g, ya, yb, yc, bonus, g, gate_bias, lnx_g, lnx_b, head_sum,
              p_a, p_b, p_c, w_out, tm):
    n, d = x.shape
    tile = lambda a: pl.BlockSpec((tm, a.shape[1]), lambda i: (i, 0))
    const = lambda a: _const_spec(a.shape)
    return pl.pallas_call(
        _merge_kernel,
        grid=(n // tm,),
        in_specs=[tile(x), const(g_norm), const(w_g), tile(ya), tile(yb), tile(yc), tile(bonus), tile(g),
                  const(gate_bias), const(lnx_g), const(lnx_b), const(head_sum),
                  const(p_a), const(p_b), const(p_c), const(w_out)],
        out_specs=pl.BlockSpec((tm, d), lambda i: (i, 0)),
        out_shape=jax.ShapeDtypeStruct((n, d), F32),
        compiler_params=_params("parallel"),
        name="merge_out",
    )(x, g_norm, w_g, ya, yb, yc, bonus, g, gate_bias, lnx_g, lnx_b, head_sum, p_a, p_b, p_c, w_out)


def _ffn_kernel(x_ref, g_ref, wg_ref, wu_ref, wd_ref, gf_ref, o_ref, h_scr, acc_scr, *, final_norm):
    j = pl.program_id(1)

    @pl.when(j == 0)
    def _():
        x = x_ref[...]
        h_scr[...] = _rmsnorm(x, g_ref[...]).astype(BF16)
        acc_scr[...] = x

    h = h_scr[...]
    act = jax.nn.silu(_dot(h, wg_ref[...])) * _dot(h, wu_ref[...])
    acc_scr[...] += _dot(act.astype(BF16), wd_ref[...])

    @pl.when(j == pl.num_programs(1) - 1)
    def _():
        y = acc_scr[...]
        o_ref[...] = _rmsnorm(y, gf_ref[...]) if final_norm else y


def ffn(x, g, w_gate_up, w_down, g_final, tm, tf, final_norm):
    n, d = x.shape
    dff = w_down.shape[0]
    nf = dff // tf
    return pl.pallas_call(
        functools.partial(_ffn_kernel, final_norm=final_norm),
        grid=(n // tm, nf),
        in_specs=[pl.BlockSpec((tm, d), lambda i, j: (i, 0)),
                  pl.BlockSpec((1, d), lambda i, j: (0, 0)),
                  pl.BlockSpec((d, tf), lambda i, j: (0, j)),
                  pl.BlockSpec((d, tf), lambda i, j: (0, j + nf)),
                  pl.BlockSpec((tf, d), lambda i, j: (j, 0)),
                  pl.BlockSpec((1, d), lambda i, j: (0, 0))],
        out_specs=pl.BlockSpec((tm, d), lambda i, j: (i, 0)),
        out_shape=jax.ShapeDtypeStruct((n, d), F32),
        scratch_shapes=[pltpu.VMEM((tm, d), BF16), pltpu.VMEM((tm, d), F32)],
        compiler_params=_params("parallel", "arbitrary"),
        name="ffn",
    )(x, g, w_gate_up, w_gate_up, w_down, g_final)


def _pad_cols(w, mult):
    pad = (-w.shape[1]) % mult
    return jnp.pad(w, ((0, 0), (0, pad))) if pad else w


def _largest_tile(total, cap):
    t = cap
    while total % t:
        t //= 2
    return t


def kernel(x, norm_mix, w_in, gate_bias, a_ln_g, a_ln_b, a_w_s, a_b_s, b_f_bias, c_mu, c_w0, c_w_up, c_a0, c_a_up, c_g_up, c_k_k, c_k_a, c_r_k, c_lnx_g, c_lnx_b, c_v0, c_v_down, c_v_up, p_a, p_b, p_c, w_out, norm_ffn, w_gate_up, w_down, norm_final):
    bn, s, d = x.shape
    depth = w_in.shape[0]
    n = bn * s
    a_width = a_ln_g.shape[1]
    b_width = B_HEADS * B_HEAD_DIM
    c_width = C_HEADS * C_HEAD_DIM
    a_cols = 2 * a_width
    b_cols = 3 * b_width + B_HEADS
    c_cols = 3 * c_width + C_DECAY_LORA + C_AAA_LORA + C_GATE_LORA
    dff = w_down.shape[1]
    assert s % 128 == 0 and dff % 256 == 0

    tm = _largest_tile(n, 512)
    t_attn = _largest_tile(s, 512)
    t_chunk = _largest_tile(s, 256)
    t_state = _largest_tile(s, 256)
    tf = dff // 2 if (dff // 2) % 128 == 0 else dff

    head_sum = jnp.kron(jnp.eye(C_HEADS, dtype=BF16), jnp.ones((C_HEAD_DIM, C_HEAD_DIM), BF16))
    row2 = lambda p: p.reshape(1, -1)

    xf = x.reshape(n, d)
    v_first = None
    for l in range(depth):
        w_l = w_in[l].astype(BF16)
        w_a = w_l[:, :a_cols]
        w_b = _pad_cols(w_l[:, a_cols:a_cols + b_cols], 128)
        w_c = w_l[:, a_cols + b_cols:a_cols + b_cols + c_cols]
        w_g = w_l[:, a_cols + b_cols + c_cols:]
        g_mix = row2(norm_mix[l])

        b_s_full = jnp.repeat(a_b_s[l].T, a_width // A_GROUPS, axis=1)
        ya = gmlp(xf, g_mix, w_a, row2(a_ln_g[l]), row2(a_ln_b[l]), a_w_s[l], b_s_full, tm)

        f_bias_row = jnp.pad(b_f_bias[l], (0, 128 - B_HEADS)).reshape(1, 128)
        qa, ka, vt = fox_prep(xf, g_mix, w_b, f_bias_row, s, tm)
        yb = fox_attention(qa, ka, vt, bn, s, tm, FOX_HEAD_GROUP)

        zeros_lora = jnp.zeros((C_AAA_LORA, c_width), F32)
        w_up_pad = jnp.concatenate([c_w_up[l], zeros_lora], axis=0)
        a_up_pad = jnp.concatenate([jnp.zeros((C_DECAY_LORA, c_width), F32), c_a_up[l]], axis=0)
        mix = {} if l == 0 else dict(v_first=v_first, v0=row2(c_v0[l - 1]),
                                     v_down=c_v_down[l - 1], v_up=c_v_up[l - 1])
        r_c, ld_c, k_c, v_c, kk_c, b_c, g_c, bonus_c = rwkv_prep(
            xf, g_mix, w_c, row2(c_mu[l]), row2(c_w0[l]), w_up_pad, row2(c_a0[l]), a_up_pad, c_g_up[l],
            row2(c_k_k[l]), row2(c_k_a[l]), row2(c_r_k[l]), head_sum, s, tm, **mix)
        if l == 0:
            v_first = v_c
        rq, y0, g_mat, h_mat = rwkv_chunk(r_c, ld_c, k_c, v_c, kk_c, b_c, t_chunk)
        per_batch = lambda t: t.reshape(bn, s, c_width)
        yc = rwkv_state(per_batch(rq), per_batch(y0), per_batch(g_mat), per_batch(h_mat), t_state)
        yc = yc.reshape(n, c_width)

        xf = merge_out(xf, g_mix, w_g, ya, yb, yc, bonus_c, g_c, gate_bias[l], row2(c_lnx_g[l]),
                       row2(c_lnx_b[l]), head_sum, p_a[l].astype(BF16), p_b[l].astype(BF16),
                       p_c[l].astype(BF16), w_out[l].astype(BF16), tm)
        xf = ffn(xf, row2(norm_ffn[l]), w_gate_up[l].astype(BF16), w_down[l].astype(BF16),
                 row2(norm_final), tm, tf, final_norm=(l == depth - 1))
    return xf.reshape(bn, s, d)
```

```python
import functools

import jax
import jax.numpy as jnp
from jax import lax
from jax.experimental import pallas as pl
from jax.experimental.pallas import tpu as pltpu

F32 = jnp.float32
BF16 = jnp.bfloat16

V7X_VMEM_LIMIT_BYTES = 56 * 1024 * 1024

NORM_EPS = 1e-6
LN_EPS = 1e-5
GN_EPS = 64e-5

N_BRANCH = 3
A_GROUPS = 4
A_CHUNK = 128
B_HEADS = 8
B_HEAD_DIM = 64
C_HEADS = 8
C_HEAD_DIM = 64
C_DECAY_LORA = 64
C_AAA_LORA = 64
C_GATE_LORA = 128
PROJECT_ROWS = 256
LOG2_E = 1.4426950408889634
DECAY_SCALE = 0.6065306597126334
FOX_HEAD_GROUP = 4
FOX_V_ROWS = 80
RWKV_CHUNK = 64
RWKV_PACK = 2


def _params(*semantics):
    return pltpu.CompilerParams(dimension_semantics=semantics,
                                vmem_limit_bytes=V7X_VMEM_LIMIT_BYTES)


def _rmsnorm(x, g):
    return x * lax.rsqrt(jnp.mean(x * x, axis=-1, keepdims=True) + NORM_EPS) * g


def _dot(a, b, precision=None):
    return jnp.dot(a, b, preferred_element_type=F32, precision=precision)


def _dot_nt(a, b, precision=None):
    return lax.dot_general(a, b, (((1,), (1,)), ((), ())),
                           preferred_element_type=F32, precision=precision)


def _bf16_terms(x, n):
    terms = []
    for _ in range(n - 1):
        t = x.astype(BF16)
        terms.append(t)
        x = x - t.astype(F32)
    terms.append(x.astype(BF16))
    return terms


def _dot_exact_rhs(x, m, n_terms):
    return sum(_dot(t, m) for t in _bf16_terms(x, n_terms))


def _dot_exact_lhs(m, x, n_terms):
    return sum(_dot(m, t) for t in _bf16_terms(x, n_terms))


def _dot_3pass(a, b):
    a_hi, a_lo = _bf16_terms(a, 2)
    b_hi, b_lo = _bf16_terms(b, 2)
    return _dot(a_hi, b_hi) + (_dot(a_lo, b_hi) + _dot(a_hi, b_lo))


def _dot_tn(a, b, precision=None):
    return lax.dot_general(a, b, (((0,), (0,)), ((), ())),
                           preferred_element_type=F32, precision=precision)


def _const_spec(shape):
    return pl.BlockSpec(shape, lambda *_: (0,) * len(shape), pipeline_mode=pl.Buffered(1))


def _project(x_ref, g_ref, w_ref, rows=slice(None)):
    return _dot(_rmsnorm(x_ref[rows, :], g_ref[...]).astype(BF16), w_ref[...])


def _interleaved(n_parts, produce, consume):
    ready = produce(0)
    for k in range(n_parts):
        upcoming = produce(k + 1) if k + 1 < n_parts else None
        consume(k, ready)
        ready = upcoming


def _gmlp_kernel(x_ref, gn_ref, w_ref, lng_ref, lnb_ref, ws_ref, bs_ref, o_ref, *, width):
    tm = x_ref.shape[0]
    gd = width // A_GROUPS
    part = min(tm, PROJECT_ROWS)
    row = lax.broadcasted_iota(jnp.int32, (A_CHUNK, A_CHUNK), 0)
    col = lax.broadcasted_iota(jnp.int32, (A_CHUNK, A_CHUNK), 1)
    causal = col <= row

    def mix(k, pa):
        for c in range(part // A_CHUNK):
            rows = slice(c * A_CHUNK, (c + 1) * A_CHUNK)
            out_rows = slice(k * part + c * A_CHUNK, k * part + (c + 1) * A_CHUNK)
            act = jax.nn.gelu(pa[rows, :])
            u = act[:, :width]
            v = act[:, width:]
            mu = jnp.mean(v, axis=-1, keepdims=True)
            var = jnp.mean(jnp.square(v - mu), axis=-1, keepdims=True)
            v = (v - mu) * lax.rsqrt(var + LN_EPS) * lng_ref[...] + lnb_ref[...]
            v = v.astype(BF16)
            for g in range(A_GROUPS):
                cols = slice(g * gd, (g + 1) * gd)
                w = jnp.where(causal, ws_ref[g], 0.0).astype(BF16)
                mixed = _dot(w, v[:, cols]) + bs_ref[:, cols]
                o_ref[out_rows, cols] = u[:, cols] * mixed

    _interleaved(tm // part,
                 lambda k: _project(x_ref, gn_ref, w_ref, slice(k * part, (k + 1) * part)), mix)


def gmlp(x, g_norm, w_a, ln_g, ln_b, w_s, b_s_full, tm):
    n, d = x.shape
    width = w_a.shape[1] // 2
    return pl.pallas_call(
        functools.partial(_gmlp_kernel, width=width),
        grid=(n // tm,),
        in_specs=[pl.BlockSpec((tm, d), lambda i: (i, 0)),
                  _const_spec((1, d)), _const_spec(w_a.shape),
                  _const_spec((1, width)), _const_spec((1, width)),
                  _const_spec((A_GROUPS, A_CHUNK, A_CHUNK)), _const_spec((A_CHUNK, width))],
        out_specs=pl.BlockSpec((tm, width), lambda i: (i, 0)),
        out_shape=jax.ShapeDtypeStruct((n, width), F32),
        compiler_params=_params("parallel"),
        name="gmlp",
    )(x, g_norm, w_a, ln_g, ln_b, w_s, b_s_full)


def _one_hot(shape, hit):
    r = lax.broadcasted_iota(jnp.int32, shape, 0)
    c = lax.broadcasted_iota(jnp.int32, shape, 1)
    return jnp.where(hit(r, c), 1.0, 0.0).astype(BF16)


def _fox_prep_kernel(x_ref, gn_ref, w_ref, fb_ref, qa_ref, ka_ref, vt_ref, carry_scr, *, seq, scale):
    tm = x_ref.shape[0]
    hd = B_HEAD_DIM
    width = B_HEADS * hd
    x = _project(x_ref, gn_ref, w_ref)

    @pl.when((pl.program_id(0) * tm) % seq == 0)
    def _():
        carry_scr[...] = jnp.zeros(carry_scr.shape, F32)

    log_f = jax.nn.log_sigmoid(x[:, 3 * width:3 * width + 128] + fb_ref[...]) * LOG2_E
    tril = (lax.broadcasted_iota(jnp.int32, (tm, tm), 1)
            <= lax.broadcasted_iota(jnp.int32, (tm, tm), 0)).astype(BF16)
    cum = _dot_exact_lhs(tril, log_f, 3) + carry_scr[...]
    carry_scr[...] = cum[tm - 1:tm, :]
    lane = lax.broadcasted_iota(jnp.int32, (tm, 128), 1)
    terms = _bf16_terms(jnp.where(lane < B_HEADS, -cum, 0.0), 3)
    c_terms = sum(pltpu.roll(t.astype(F32), B_HEADS * i, axis=1) if i else t.astype(F32)
                  for i, t in enumerate(terms)).astype(BF16)
    place_c = _one_hot((128, B_HEADS * 128),
                       lambda r, c: (r < 3 * B_HEADS) & (c == 128 * (r % B_HEADS) + hd + r // B_HEADS))
    c_placed = _dot(c_terms, place_c)

    low = lane < hd
    ones_lanes = (lane >= hd) & (lane < hd + 3)
    v_rows_n = vt_ref.shape[2]
    row_t = lax.broadcasted_iota(jnp.int32, (v_rows_n, tm), 0)
    for pair in range(B_HEADS // 2):
        lanes = slice(pair * 128, pair * 128 + 128)
        q = x[:, lanes] * (scale * LOG2_E)
        k = x[:, width:2 * width][:, lanes]
        v = x[:, 2 * width:3 * width][:, lanes].astype(BF16)
        for odd in range(2):
            h = 2 * pair + odd
            q_h, k_h = (pltpu.roll(q, hd, axis=1), pltpu.roll(k, hd, axis=1)) if odd else (q, k)
            v_rows = _one_hot((v_rows_n, 128), lambda r, c: (r < hd) & (c == r + odd * hd))
            qa_ref[h] = jnp.where(ones_lanes, 1.0, jnp.where(low, q_h, 0.0)).astype(BF16)
            ka_ref[h] = (jnp.where(low, k_h, 0.0) + c_placed[:, h * 128:(h + 1) * 128]).astype(BF16)
            vt_ref[h, 0] = jnp.where(row_t == hd, 1.0, _dot_nt(v_rows, v)).astype(BF16)


def fox_prep(x, g_norm, w_b, f_bias_row, seq, tm):
    n, d = x.shape
    return pl.pallas_call(
        functools.partial(_fox_prep_kernel, seq=seq, scale=B_HEAD_DIM ** -0.5),
        grid=(n // tm,),
        in_specs=[pl.BlockSpec((tm, d), lambda i: (i, 0)),
                  _const_spec((1, d)), _const_spec(w_b.shape), _const_spec((1, 128))],
        out_specs=[pl.BlockSpec((B_HEADS, tm, 128), lambda i: (0, i, 0)),
                   pl.BlockSpec((B_HEADS, tm, 128), lambda i: (0, i, 0)),
                   pl.BlockSpec((B_HEADS, 1, FOX_V_ROWS, tm), lambda i: (0, i, 0, 0))],
        out_shape=[jax.ShapeDtypeStruct((B_HEADS, n, 128), BF16),
                   jax.ShapeDtypeStruct((B_HEADS, n, 128), BF16),
                   jax.ShapeDtypeStruct((B_HEADS, n // tm, FOX_V_ROWS, tm), BF16)],
        scratch_shapes=[pltpu.VMEM((1, 128), F32)],
        compiler_params=_params("arbitrary"),
        name="fox_prep",
    )(x, g_norm, w_b, f_bias_row)


def _fox_kernel(qa_ref, ka_ref, vt_ref, o_ref, m_scr, acc_scr, s_scr, smax_scr):
    i = pl.program_id(2)
    heads, tq, _ = qa_ref.shape
    tk = vt_ref.shape[3]
    hd = B_HEAD_DIM
    m_scr[...] = jnp.full(m_scr.shape, -jnp.inf, F32)
    acc_scr[...] = jnp.zeros(acc_scr.shape, F32)

    hs = range(heads)

    def logits(j, slot, h):
        start = pl.multiple_of(j * tk, tk)
        s = _dot_nt(ka_ref[h, pl.ds(start, tk), :], qa_ref[h])
        s_scr[slot, h] = s
        smax_scr[slot, h] = jnp.max(s, axis=0, keepdims=True)

    def consume(slot, j, h, masked):
        s = s_scr[slot, h]
        if masked:
            key = lax.broadcasted_iota(jnp.int32, (tk, tq), 0)
            qry = lax.broadcasted_iota(jnp.int32, (tk, tq), 1)
            s = jnp.where(key <= qry, s, -jnp.inf)
            s_max = jnp.max(s, axis=0, keepdims=True)
        else:
            s_max = smax_scr[slot, h]
        m_prev = m_scr[h]
        m_new = jnp.maximum(m_prev, s_max)
        p = jnp.exp2((s - m_new).astype(BF16))
        acc_scr[h] = jnp.exp2(m_prev - m_new) * acc_scr[h] + _dot(vt_ref[h, j], p)
        m_scr[h] = m_new

    def advance(cur, cur_slot, masked=False, nxt=None):
        for h in hs:
            if nxt is not None:
                logits(nxt, 1 - cur_slot, h)
            consume(cur_slot, cur, h, masked)

    for h in hs:
        logits(0, 0, h)

    def body(jj, carry):
        a = 2 * jj
        advance(a, 0, nxt=a + 1)
        advance(a + 1, 1, nxt=a + 2)
        return carry

    lax.fori_loop(0, i // 2, body, 0)

    @pl.when(i % 2 == 0)
    def _():
        advance(i, 0, masked=True)

    @pl.when(i % 2 == 1)
    def _():
        advance(i - 1, 0, nxt=i)
        advance(i, 1, masked=True)

    out_t = [acc_scr[h, 0:hd, :] / acc_scr[h, hd:hd + 1, :] for h in range(heads)]
    o_ref[...] = jnp.concatenate(out_t, axis=0).T


def fox_attention(qa, ka, vt, bn, seq, tq, group):
    heads, n, _ = qa.shape
    rows = vt.shape[2]
    nq = seq // tq
    return pl.pallas_call(
        _fox_kernel,
        grid=(bn, heads // group, nq),
        in_specs=[pl.BlockSpec((group, tq, 128), lambda bi, hi, i: (hi, bi * nq + i, 0)),
                  pl.BlockSpec((group, seq, 128), lambda bi, hi, i: (hi, bi, 0)),
                  pl.BlockSpec((group, nq, rows, tq), lambda bi, hi, i: (hi, bi, 0, 0))],
        out_specs=pl.BlockSpec((tq, group * B_HEAD_DIM), lambda bi, hi, i: (bi * nq + i, hi)),
        out_shape=jax.ShapeDtypeStruct((n, heads * B_HEAD_DIM), F32),
        scratch_shapes=[pltpu.VMEM((group, 1, tq), F32), pltpu.VMEM((group, rows, tq), F32),
                        pltpu.VMEM((2, group, tq, tq), F32), pltpu.VMEM((2, group, 1, tq), F32)],
        compiler_params=_params("parallel", "parallel", "arbitrary"),
        name="fox_attention",
    )(qa, ka, vt)


def _rwkv_prep_kernel(*refs, seq, mix_value):
    (x_ref, gn_ref, w_ref, mu_ref, w0_ref, wup_ref, a0_ref, aup_ref, gup_ref, kk_ref, ka_ref,
     rk_ref, hsum_ref) = refs[:13]
    if mix_value:
        vfirst_ref, v0_ref, vdown_ref, vup_ref = refs[13:17]
    r_out, ld_out, k_out, v_out, kk_out, b_out, g_out, bonus_out, last_scr = refs[-9:]
    tm = x_ref.shape[0]
    cw = r_out.shape[1]
    part = min(tm, PROJECT_ROWS)

    @pl.when((pl.program_id(0) * tm) % seq == 0)
    def _():
        last_scr[...] = jnp.zeros(last_scr.shape, F32)

    def derive(part_idx, pc):
        rows = slice(part_idx * part, (part_idx + 1) * part)
        prev_row = last_scr[...]
        last_scr[...] = pc[part - 1:part, :]
        row = lax.broadcasted_iota(jnp.int32, pc.shape, 0)
        shifted = jnp.where(row == 0, prev_row, pltpu.roll(pc, 1, axis=0))
        pc = pc + (shifted - pc) * mu_ref[...]

        r = pc[:, 0:cw]
        k = pc[:, cw:2 * cw]
        v = pc[:, 2 * cw:3 * cw]
        lora = pc[:, 3 * cw:3 * cw + C_DECAY_LORA + C_AAA_LORA]
        g_lo = pc[:, 3 * cw + C_DECAY_LORA + C_AAA_LORA:]
        if mix_value:
            lam = jax.nn.sigmoid(v0_ref[...] + _mm(_mm(v, vdown_ref[...]), vup_ref[...]))
            v = v + (vfirst_ref[rows, :] - v) * lam
        z = w0_ref[...] + _dot_3pass(jnp.tanh(lora), wup_ref[...])
        log_decay = -DECAY_SCALE * jax.nn.sigmoid(z)
        a = jax.nn.sigmoid(a0_ref[...] + _mm(lora, aup_ref[...]))
        g = _mm(jax.nn.sigmoid(g_lo), gup_ref[...])
        kk = k * kk_ref[...]
        ss = _dot_exact_rhs(kk * kk, hsum_ref[...], 1)
        kk = kk / jnp.maximum(jnp.sqrt(ss), 1e-12)
        k = k * (1.0 + (a - 1.0) * ka_ref[...])
        r_out[rows, :] = r
        ld_out[rows, :] = log_decay
        k_out[rows, :] = k
        v_out[rows, :] = v
        kk_out[rows, :] = kk
        b_out[rows, :] = kk * a
        g_out[rows, :] = g
        bonus_out[rows, :] = _dot_exact_rhs(r * k * rk_ref[...], hsum_ref[...], 1) * v

    _interleaved(tm // part,
                 lambda k: _project(x_ref, gn_ref, w_ref, slice(k * part, (k + 1) * part)), derive)


def rwkv_prep(x, g_norm, w_c, mu, w0, w_up_pad, a0, a_up_pad, g_up, k_k, k_a, r_k, head_sum, seq, tm,
              v_first=None, v0=None, v_down=None, v_up=None):
    n, d = x.shape
    ccols = w_c.shape[1]
    cw = w0.shape[1]
    mix_value = v_first is not None
    const = _const_spec
    in_specs = [pl.BlockSpec((tm, d), lambda i: (i, 0)), const((1, d)), const(w_c.shape),
                const((1, ccols)), const((1, cw)), const(w_up_pad.shape), const((1, cw)),
                const(a_up_pad.shape), const(g_up.shape), const((1, cw)), const((1, cw)),
                const((1, cw)), const(head_sum.shape)]
    args = [x, g_norm, w_c, mu, w0, w_up_pad, a0, a_up_pad, g_up, k_k, k_a, r_k, head_sum]
    if mix_value:
        in_specs += [pl.BlockSpec((tm, cw), lambda i: (i, 0)), const((1, cw)),
                     const(v_down.shape), const(v_up.shape)]
        args += [v_first, v0, v_down, v_up]
    out = jax.ShapeDtypeStruct((n, cw), F32)
    return pl.pallas_call(
        functools.partial(_rwkv_prep_kernel, seq=seq, mix_value=mix_value),
        grid=(n // tm,),
        in_specs=in_specs,
        out_specs=[pl.BlockSpec((tm, cw), lambda i: (i, 0))] * 8,
        out_shape=[out] * 8,
        scratch_shapes=[pltpu.VMEM((1, ccols), F32)],
        compiler_params=_params("arbitrary"),
        name="rwkv_prep",
    )(*args)


def _lane_head(shape):
    return lax.broadcasted_iota(jnp.int32, shape, 1) // C_HEAD_DIM


def _stack_masked(x, pack):
    head = _lane_head(x.shape)
    return jnp.concatenate([jnp.where(head == h, x, 0.0) for h in range(pack)], axis=0)


def _diag_blocks(full, pack):
    n = C_HEAD_DIM
    head = _lane_head((n, pack * n))
    out = full[0:n]
    for h in range(1, pack):
        out = jnp.where(head == h, full[h * n:(h + 1) * n], out)
    return out


def _mm(a, b):
    return _dot(a.astype(BF16), b.astype(BF16))


def _mm_nt(a, b):
    return _dot_nt(a.astype(BF16), b.astype(BF16))


def _mm_tn(a, b):
    return _dot_tn(a.astype(BF16), b.astype(BF16))


def _rwkv_chunk_kernel(r_ref, ld_ref, k_ref, v_ref, kk_ref, b_ref,
                       rq_out, y0_out, g_out, h_out, *, pack):
    tc, cw = r_ref.shape
    c = RWKV_CHUNK
    pw = pack * C_HEAD_DIM
    row = lax.broadcasted_iota(jnp.int32, (c, pw), 0)
    col = lax.broadcasted_iota(jnp.int32, (c, pw), 1) % C_HEAD_DIM
    incl = col <= row
    strict = col < row
    eye = (col == row).astype(F32)
    tril_ones = (lax.broadcasted_iota(jnp.int32, (c, c), 1)
                 <= lax.broadcasted_iota(jnp.int32, (c, c), 0)).astype(BF16)

    units = [(slice(ci * c, (ci + 1) * c), slice(p * pw, (p + 1) * pw))
             for ci in range(tc // c) for p in range(cw // pw)]
    each = lambda fn, *lists: [fn(*args) for args in zip(*lists)]
    stack = lambda x: _stack_masked(x, pack)
    diag = lambda x: _diag_blocks(x, pack)

    cums = {}
    for rows, _ in units:
        if rows.start not in cums:
            cums[rows.start] = _dot_exact_lhs(tril_ones, ld_ref[rows, :], 3)
    cum = [cums[rows.start][:, lanes] for rows, lanes in units]
    ld = [ld_ref[u] for u in units]
    v = [v_ref[u] for u in units]
    p_incl = each(jnp.exp, cum)
    p_inv = each(lambda x: jnp.exp(-x), cum)
    p_prev = each(lambda x, y: jnp.exp(x - y), cum, ld)
    r_t = each(lambda u, s: r_ref[u] * s, units, p_incl)
    kk_t = each(lambda u, s: kk_ref[u] * s, units, p_prev)
    b_t = each(lambda u, s: b_ref[u] * s, units, p_inv)
    k_t = each(lambda u, s: k_ref[u] * s, units, p_inv)

    lhs = each(lambda x, y: jnp.concatenate([x, y], axis=0).astype(BF16), kk_t, r_t)
    with_b = each(lambda x, y: _dot_nt(x, stack(y).astype(BF16)), lhs, b_t)
    with_k = each(lambda x, y: _dot_nt(x, stack(y).astype(BF16)), lhs, k_t)
    a_b = each(lambda x: jnp.where(strict, x[:c], 0.0), with_b)
    a_rb = each(lambda x: jnp.where(incl, x[c:], 0.0).astype(BF16), with_b)
    a_kk = each(lambda x: jnp.concatenate([jnp.where(strict, x[:c], 0.0),
                                           jnp.where(incl, x[c:], 0.0)], axis=0), with_k)
    av = each(lambda x, y: _mm(x, stack(y)), a_kk, v)

    lower_left = lambda m: (((row // (2 * m)) == (col // (2 * m)))
                            & ((row // m) % 2 == 1) & ((col // m) % 2 == 0))
    t_inv = each(lambda x: eye - jnp.where(lower_left(1), x, 0.0), a_b)
    m = 2
    while m < c:
        mask = lower_left(m)
        xl = each(lambda x, y: _mm(x, stack(jnp.where(mask, y, 0.0))), t_inv, a_b)
        t_inv = each(lambda x, y: x - _mm(y, stack(x)), t_inv, xl)
        m *= 2

    t_bf = each(lambda x: x.astype(BF16), t_inv)
    w = each(lambda x, y: _dot(x, stack(y).astype(BF16)), t_bf, kk_t)
    u0 = each(lambda x, y: _dot(x, stack(y[:c]).astype(BF16)), t_bf, av)
    for i, u in enumerate(units):
        rq_out[u] = r_t[i] - _dot(a_rb[i], stack(w[i]).astype(BF16))
        y0_out[u] = av[i][c:] - _dot(a_rb[i], stack(u0[i]).astype(BF16))
    for i, u in enumerate(units):
        p_end = p_incl[i][c - 1:c, :]
        g_out[u] = (eye - diag(_mm_tn(w[i], b_t[i]))) * p_end
        h_out[u] = (diag(_mm_tn(v[i], k_t[i])) - diag(_mm_tn(u0[i], b_t[i]))) * p_end


def rwkv_chunk(r, ld, k, v, kk, b, tc):
    n, cw = r.shape
    spec = pl.BlockSpec((tc, cw), lambda i: (i, 0))
    out = jax.ShapeDtypeStruct((n, cw), F32)
    return pl.pallas_call(
        functools.partial(_rwkv_chunk_kernel, pack=RWKV_PACK),
        grid=(n // tc,),
        in_specs=[spec] * 6,
        out_specs=[spec] * 4,
        out_shape=[out] * 4,
        compiler_params=_params("parallel"),
        name="rwkv_chunk",
    )(r, ld, k, v, kk, b)


def _rwkv_state_kernel(rq_ref, y0_ref, g_ref, h_ref, y_out, state_scr, *, pack):
    bn, tc, cw = rq_ref.shape
    c = RWKV_CHUNK
    pw = pack * C_HEAD_DIM
    groups = cw // pw

    @pl.when(pl.program_id(0) == 0)
    def _():
        state_scr[...] = jnp.zeros(state_scr.shape, F32)

    chains = [(bi, p, slice(p * pw, (p + 1) * pw)) for bi in range(bn) for p in range(groups)]
    state = [state_scr[bi, p] for bi, p, _ in chains]
    for ci in range(tc // c):
        rows = slice(ci * c, (ci + 1) * c)
        for i, (bi, p, lanes) in enumerate(chains):
            y_out[bi, rows, lanes] = _mm_nt(rq_ref[bi, rows, lanes], state[i]) + y0_ref[bi, rows, lanes]
        for i, (bi, p, lanes) in enumerate(chains):
            g_bd = _stack_masked(g_ref[bi, rows, lanes], pack)
            h_bd = _stack_masked(h_ref[bi, rows, lanes], pack)
            state[i] = _mm(state[i], g_bd) + h_bd
    for i, (bi, p, _) in enumerate(chains):
        state_scr[bi, p] = state[i]


def rwkv_state(rq, y0, g, h, tc):
    bn, s, cw = rq.shape
    pw = RWKV_PACK * C_HEAD_DIM
    spec = pl.BlockSpec((bn, tc, cw), lambda t: (0, t, 0))
    return pl.pallas_call(
        functools.partial(_rwkv_state_kernel, pack=RWKV_PACK),
        grid=(s // tc,),
        in_specs=[spec] * 4,
        out_specs=spec,
        out_shape=jax.ShapeDtypeStruct(rq.shape, F32),
        scratch_shapes=[pltpu.VMEM((bn, cw // pw, pw, pw), F32)],
        compiler_params=_params("arbitrary"),
        name="rwkv_state",
    )(rq, y0, g, h)


def _merge_kernel(x_ref, gn_ref, wg_ref, ya_ref, yb_ref, yc_ref, bonus_ref, g_ref, gb_ref, lng_ref, lnb_ref,
                  hsum_ref, pa_ref, pb_ref, pc_ref, wo_ref, o_ref):
    d = x_ref.shape[1]
    x = x_ref[...]
    h = _rmsnorm(x, gn_ref[...]).astype(BF16)
    y = yc_ref[...]
    inv_n = 1.0 / C_HEAD_DIM
    mu = _dot_exact_rhs(y, hsum_ref[...], 1) * inv_n
    yc = y - mu
    var = _dot_exact_rhs(yc * yc, hsum_ref[...], 1) * inv_n
    yc = yc * lax.rsqrt(var + GN_EPS) * lng_ref[...] + lnb_ref[...]
    yc = (yc + bonus_ref[...]) * g_ref[...]
    branches = ((ya_ref, pa_ref), (yb_ref, pb_ref), (yc, pc_ref))
    merged = None
    for bi, (y_b, p_ref) in enumerate(branches):
        y_b = y_b if bi == 2 else y_b[...]
        gate = jax.nn.sigmoid(_dot(h, wg_ref[:, bi * d:(bi + 1) * d]) + gb_ref[bi:bi + 1, :])
        term = gate * _dot(y_b.astype(BF16), p_ref[...])
        merged = term if merged is None else merged + term
    o_ref[...] = x + _dot(merged.astype(BF16), wo_ref[...])


def merge_out(x, g_norm, w_g, ya, yb, yc, bonus, g, gate_bias, lnx_g, lnx_b, head_sum,
              p_a, p_b, p_c, w_out, tm):
    n, d = x.shape
    tile = lambda a: pl.BlockSpec((tm, a.shape[1]), lambda i: (i, 0))
    const = lambda a: _const_spec(a.shape)
    return pl.pallas_call(
        _merge_kernel,
        grid=(n // tm,),
        in_specs=[tile(x), const(g_norm), const(w_g), tile(ya), tile(yb), tile(yc), tile(bonus), tile(g),
                  const(gate_bias), const(lnx_g), const(lnx_b), const(head_sum),
                  const(p_a), const(p_b), const(p_c), const(w_out)],
        out_specs=pl.BlockSpec((tm, d), lambda i: (i, 0)),
        out_shape=jax.ShapeDtypeStruct((n, d), F32),
        compiler_params=_params("parallel"),
        name="merge_out",
    )(x, g_norm, w_g, ya, yb, yc, bonus, g, gate_bias, lnx_g, lnx_b, head_sum, p_a, p_b, p_c, w_out)


def _ffn_kernel(x_ref, g_ref, wg_ref, wu_ref, wd_ref, gf_ref, o_ref, h_scr, acc_scr, *, final_norm):
    j = pl.program_id(1)

    @pl.when(j == 0)
    def _():
        x = x_ref[...]
        h_scr[...] = _rmsnorm(x, g_ref[...]).astype(BF16)
        acc_scr[...] = x

    h = h_scr[...]
    act = jax.nn.silu(_dot(h, wg_ref[...])) * _dot(h, wu_ref[...])
    acc_scr[...] += _dot(act.astype(BF16), wd_ref[...])

    @pl.when(j == pl.num_programs(1) - 1)
    def _():
        y = acc_scr[...]
        o_ref[...] = _rmsnorm(y, gf_ref[...]) if final_norm else y


def ffn(x, g, w_gate_up, w_down, g_final, tm, tf, final_norm):
    n, d = x.shape
    dff = w_down.shape[0]
    nf = dff // tf
    return pl.pallas_call(
        functools.partial(_ffn_kernel, final_norm=final_norm),
        grid=(n // tm, nf),
        in_specs=[pl.BlockSpec((tm, d), lambda i, j: (i, 0)),
                  pl.BlockSpec((1, d), lambda i, j: (0, 0)),
                  pl.BlockSpec((d, tf), lambda i, j: (0, j)),
                  pl.BlockSpec((d, tf), lambda i, j: (0, j + nf)),
                  pl.BlockSpec((tf, d), lambda i, j: (j, 0)),
                  pl.BlockSpec((1, d), lambda i, j: (0, 0))],
        out_specs=pl.BlockSpec((tm, d), lambda i, j: (i, 0)),
        out_shape=jax.ShapeDtypeStruct((n, d), F32),
        scratch_shapes=[pltpu.VMEM((tm, d), BF16), pltpu.VMEM((tm, d), F32)],
        compiler_params=_params("parallel", "arbitrary"),
        name="ffn",
    )(x, g, w_gate_up, w_gate_up, w_down, g_final)


def _pad_cols(w, mult):
    pad = (-w.shape[1]) % mult
    return jnp.pad(w, ((0, 0), (0, pad))) if pad else w


def _largest_tile(total, cap):
    t = cap
    while total % t:
        t //= 2
    return t


def kernel(x, norm_mix, w_in, gate_bias, a_ln_g, a_ln_b, a_w_s, a_b_s, b_f_bias, c_mu, c_w0, c_w_up, c_a0, c_a_up, c_g_up, c_k_k, c_k_a, c_r_k, c_lnx_g, c_lnx_b, c_v0, c_v_down, c_v_up, p_a, p_b, p_c, w_out, norm_ffn, w_gate_up, w_down, norm_final):
    bn, s, d = x.shape
    depth = w_in.shape[0]
    n = bn * s
    a_width = a_ln_g.shape[1]
    b_width = B_HEADS * B_HEAD_DIM
    c_width = C_HEADS * C_HEAD_DIM
    a_cols = 2 * a_width
    b_cols = 3 * b_width + B_HEADS
    c_cols = 3 * c_width + C_DECAY_LORA + C_AAA_LORA + C_GATE_LORA
    dff = w_down.shape[1]
    assert s % 128 == 0 and dff % 256 == 0

    tm = _largest_tile(n, 512)
    t_attn = _largest_tile(s, 512)
    t_chunk = _largest_tile(s, 256)
    t_state = _largest_tile(s, 256)
    tf = dff // 2 if (dff // 2) % 128 == 0 else dff

    head_sum = jnp.kron(jnp.eye(C_HEADS, dtype=BF16), jnp.ones((C_HEAD_DIM, C_HEAD_DIM), BF16))
    row2 = lambda p: p.reshape(1, -1)

    xf = x.reshape(n, d)
    v_first = None
    for l in range(depth):
        w_a = w_in[l, :, :a_cols].astype(BF16)
        w_b = _pad_cols(w_in[l, :, a_cols:a_cols + b_cols].astype(BF16), 128)
        w_c = w_in[l, :, a_cols + b_cols:a_cols + b_cols + c_cols].astype(BF16)
        w_g = w_in[l, :, a_cols + b_cols + c_cols:].astype(BF16)
        g_mix = row2(norm_mix[l])

        b_s_full = jnp.repeat(a_b_s[l].T, a_width // A_GROUPS, axis=1)
        ya = gmlp(xf, g_mix, w_a, row2(a_ln_g[l]), row2(a_ln_b[l]), a_w_s[l], b_s_full, tm)

        f_bias_row = jnp.pad(b_f_bias[l], (0, 128 - B_HEADS)).reshape(1, 128)
        qa, ka, vt = fox_prep(xf, g_mix, w_b, f_bias_row, s, tm)
        yb = fox_attention(qa, ka, vt, bn, s, tm, FOX_HEAD_GROUP)

        zeros_lora = jnp.zeros((C_AAA_LORA, c_width), F32)
        w_up_pad = jnp.concatenate([c_w_up[l], zeros_lora], axis=0)
        a_up_pad = jnp.concatenate([jnp.zeros((C_DECAY_LORA, c_width), F32), c_a_up[l]], axis=0)
        mix = {} if l == 0 else dict(v_first=v_first, v0=row2(c_v0[l - 1]),
                                     v_down=c_v_down[l - 1], v_up=c_v_up[l - 1])
        r_c, ld_c, k_c, v_c, kk_c, b_c, g_c, bonus_c = rwkv_prep(
            xf, g_mix, w_c, row2(c_mu[l]), row2(c_w0[l]), w_up_pad, row2(c_a0[l]), a_up_pad, c_g_up[l],
            row2(c_k_k[l]), row2(c_k_a[l]), row2(c_r_k[l]), head_sum, s, tm, **mix)
        if l == 0:
            v_first = v_c
        rq, y0, g_mat, h_mat = rwkv_chunk(r_c, ld_c, k_c, v_c, kk_c, b_c, t_chunk)
        per_batch = lambda t: t.reshape(bn, s, c_width)
        yc = rwkv_state(per_batch(rq), per_batch(y0), per_batch(g_mat), per_batch(h_mat), t_state)
        yc = yc.reshape(n, c_width)

        xf = merge_out(xf, g_mix, w_g, ya, yb, yc, bonus_c, g_c, gate_bias[l], row2(c_lnx_g[l]),
                       row2(c_lnx_b[l]), head_sum, p_a[l].astype(BF16), p_b[l].astype(BF16),
                       p_c[l].astype(BF16), w_out[l].astype(BF16), tm)
        xf = ffn(xf, row2(norm_ffn[l]), w_gate_up[l].astype(BF16), w_down[l].astype(BF16),
                 row2(norm_final), _largest_tile(n, 1024), tf, final_norm=(l == depth - 1))
    return xf.reshape(bn, s, d)
```

```python
import functools

import jax
import jax.numpy as jnp
from jax import lax
from jax.experimental import pallas as pl
from jax.experimental.pallas import tpu as pltpu

F32 = jnp.float32
BF16 = jnp.bfloat16

V7X_VMEM_LIMIT_BYTES = 56 * 1024 * 1024

NORM_EPS = 1e-6
LN_EPS = 1e-5
GN_EPS = 64e-5

N_BRANCH = 3
A_GROUPS = 4
A_CHUNK = 128
B_HEADS = 8
B_HEAD_DIM = 64
C_HEADS = 8
C_HEAD_DIM = 64
C_DECAY_LORA = 64
C_AAA_LORA = 64
C_GATE_LORA = 128
PROJECT_ROWS = 256
LOG2_E = 1.4426950408889634
DECAY_SCALE = 0.6065306597126334
FOX_HEAD_GROUP = 4
FOX_V_ROWS = 80
RWKV_CHUNK = 64
RWKV_PACK = 2


def _params(*semantics):
    return pltpu.CompilerParams(dimension_semantics=semantics,
                                vmem_limit_bytes=V7X_VMEM_LIMIT_BYTES)


def _rmsnorm(x, g):
    return x * lax.rsqrt(jnp.mean(x * x, axis=-1, keepdims=True) + NORM_EPS) * g


def _dot(a, b, precision=None):
    return jnp.dot(a, b, preferred_element_type=F32, precision=precision)


def _dot_nt(a, b, precision=None):
    return lax.dot_general(a, b, (((1,), (1,)), ((), ())),
                           preferred_element_type=F32, precision=precision)


def _bf16_terms(x, n):
    terms = []
    for _ in range(n - 1):
        t = x.astype(BF16)
        terms.append(t)
        x = x - t.astype(F32)
    terms.append(x.astype(BF16))
    return terms


def _dot_exact_rhs(x, m, n_terms):
    return sum(_dot(t, m) for t in _bf16_terms(x, n_terms))


def _dot_exact_lhs(m, x, n_terms):
    return sum(_dot(m, t) for t in _bf16_terms(x, n_terms))


def _dot_3pass(a, b):
    a_hi, a_lo = _bf16_terms(a, 2)
    b_hi, b_lo = _bf16_terms(b, 2)
    return _dot(a_hi, b_hi) + (_dot(a_lo, b_hi) + _dot(a_hi, b_lo))


def _dot_tn(a, b, precision=None):
    return lax.dot_general(a, b, (((0,), (0,)), ((), ())),
                           preferred_element_type=F32, precision=precision)


def _const_spec(shape):
    return pl.BlockSpec(shape, lambda *_: (0,) * len(shape), pipeline_mode=pl.Buffered(1))


def _project(x_ref, g_ref, wt_ref, rows=slice(None)):
    return _dot_nt(_rmsnorm(x_ref[rows, :], g_ref[...]).astype(BF16), wt_ref[...])


def _interleaved(n_parts, produce, consume):
    ready = produce(0)
    for k in range(n_parts):
        upcoming = produce(k + 1) if k + 1 < n_parts else None
        consume(k, ready)
        ready = upcoming


def _gmlp_kernel(x_ref, gn_ref, w_ref, lng_ref, lnb_ref, ws_ref, bs_ref, o_ref, *, width):
    tm = x_ref.shape[0]
    gd = width // A_GROUPS
    part = min(tm, PROJECT_ROWS)
    row = lax.broadcasted_iota(jnp.int32, (A_CHUNK, A_CHUNK), 0)
    col = lax.broadcasted_iota(jnp.int32, (A_CHUNK, A_CHUNK), 1)
    causal = col <= row

    def mix(k, pa):
        for c in range(part // A_CHUNK):
            rows = slice(c * A_CHUNK, (c + 1) * A_CHUNK)
            out_rows = slice(k * part + c * A_CHUNK, k * part + (c + 1) * A_CHUNK)
            act = jax.nn.gelu(pa[rows, :])
            u = act[:, :width]
            v = act[:, width:]
            mu = jnp.mean(v, axis=-1, keepdims=True)
            var = jnp.mean(jnp.square(v - mu), axis=-1, keepdims=True)
            v = (v - mu) * lax.rsqrt(var + LN_EPS) * lng_ref[...] + lnb_ref[...]
            v = v.astype(BF16)
            for g in range(A_GROUPS):
                cols = slice(g * gd, (g + 1) * gd)
                w = jnp.where(causal, ws_ref[g], 0.0).astype(BF16)
                mixed = _dot(w, v[:, cols]) + bs_ref[:, cols]
                o_ref[out_rows, cols] = u[:, cols] * mixed

    _interleaved(tm // part,
                 lambda k: _project(x_ref, gn_ref, w_ref, slice(k * part, (k + 1) * part)), mix)


def gmlp(x, g_norm, w_a, ln_g, ln_b, w_s, b_s_full, tm):
    n, d = x.shape
    width = w_a.shape[0] // 2
    return pl.pallas_call(
        functools.partial(_gmlp_kernel, width=width),
        grid=(n // tm,),
        in_specs=[pl.BlockSpec((tm, d), lambda i: (i, 0)),
                  _const_spec((1, d)), _const_spec(w_a.shape),
                  _const_spec((1, width)), _const_spec((1, width)),
                  _const_spec((A_GROUPS, A_CHUNK, A_CHUNK)), _const_spec((A_CHUNK, width))],
        out_specs=pl.BlockSpec((tm, width), lambda i: (i, 0)),
        out_shape=jax.ShapeDtypeStruct((n, width), F32),
        compiler_params=_params("parallel"),
        name="gmlp",
    )(x, g_norm, w_a, ln_g, ln_b, w_s, b_s_full)


def _one_hot(shape, hit):
    r = lax.broadcasted_iota(jnp.int32, shape, 0)
    c = lax.broadcasted_iota(jnp.int32, shape, 1)
    return jnp.where(hit(r, c), 1.0, 0.0).astype(BF16)


def _fox_prep_kernel(x_ref, gn_ref, w_ref, fb_ref, qa_ref, ka_ref, vt_ref, carry_scr, *, seq, scale):
    tm = x_ref.shape[0]
    hd = B_HEAD_DIM
    width = B_HEADS * hd
    x = _project(x_ref, gn_ref, w_ref)

    @pl.when((pl.program_id(0) * tm) % seq == 0)
    def _():
        carry_scr[...] = jnp.zeros(carry_scr.shape, F32)

    log_f = jax.nn.log_sigmoid(x[:, 3 * width:3 * width + 128] + fb_ref[...]) * LOG2_E
    tril = (lax.broadcasted_iota(jnp.int32, (tm, tm), 1)
            <= lax.broadcasted_iota(jnp.int32, (tm, tm), 0)).astype(BF16)
    cum = _dot_exact_lhs(tril, log_f, 3) + carry_scr[...]
    carry_scr[...] = cum[tm - 1:tm, :]
    lane = lax.broadcasted_iota(jnp.int32, (tm, 128), 1)
    terms = _bf16_terms(jnp.where(lane < B_HEADS, -cum, 0.0), 3)
    c_terms = sum(pltpu.roll(t.astype(F32), B_HEADS * i, axis=1) if i else t.astype(F32)
                  for i, t in enumerate(terms)).astype(BF16)
    place_c = _one_hot((128, B_HEADS * 128),
                       lambda r, c: (r < 3 * B_HEADS) & (c == 128 * (r % B_HEADS) + hd + r // B_HEADS))
    c_placed = _dot(c_terms, place_c)

    low = lane < hd
    ones_lanes = (lane >= hd) & (lane < hd + 3)
    v_rows_n = vt_ref.shape[2]
    row_t = lax.broadcasted_iota(jnp.int32, (v_rows_n, tm), 0)
    for pair in range(B_HEADS // 2):
        lanes = slice(pair * 128, pair * 128 + 128)
        q = x[:, lanes] * (scale * LOG2_E)
        k = x[:, width:2 * width][:, lanes]
        v = x[:, 2 * width:3 * width][:, lanes].astype(BF16)
        for odd in range(2):
            h = 2 * pair + odd
            q_h, k_h = (pltpu.roll(q, hd, axis=1), pltpu.roll(k, hd, axis=1)) if odd else (q, k)
            v_rows = _one_hot((v_rows_n, 128), lambda r, c: (r < hd) & (c == r + odd * hd))
            qa_ref[h] = jnp.where(ones_lanes, 1.0, jnp.where(low, q_h, 0.0)).astype(BF16)
            ka_ref[h] = (jnp.where(low, k_h, 0.0) + c_placed[:, h * 128:(h + 1) * 128]).astype(BF16)
            vt_ref[h, 0] = jnp.where(row_t == hd, 1.0, _dot_nt(v_rows, v)).astype(BF16)


def fox_prep(x, g_norm, w_b, f_bias_row, seq, tm):
    n, d = x.shape
    return pl.pallas_call(
        functools.partial(_fox_prep_kernel, seq=seq, scale=B_HEAD_DIM ** -0.5),
        grid=(n // tm,),
        in_specs=[pl.BlockSpec((tm, d), lambda i: (i, 0)),
                  _const_spec((1, d)), _const_spec(w_b.shape), _const_spec((1, 128))],
        out_specs=[pl.BlockSpec((B_HEADS, tm, 128), lambda i: (0, i, 0)),
                   pl.BlockSpec((B_HEADS, tm, 128), lambda i: (0, i, 0)),
                   pl.BlockSpec((B_HEADS, 1, FOX_V_ROWS, tm), lambda i: (0, i, 0, 0))],
        out_shape=[jax.ShapeDtypeStruct((B_HEADS, n, 128), BF16),
                   jax.ShapeDtypeStruct((B_HEADS, n, 128), BF16),
                   jax.ShapeDtypeStruct((B_HEADS, n // tm, FOX_V_ROWS, tm), BF16)],
        scratch_shapes=[pltpu.VMEM((1, 128), F32)],
        compiler_params=_params("arbitrary"),
        name="fox_prep",
    )(x, g_norm, w_b, f_bias_row)


def _fox_kernel(qa_ref, ka_ref, vt_ref, o_ref, m_scr, acc_scr, s_scr, smax_scr):
    i = pl.program_id(2)
    heads, tq, _ = qa_ref.shape
    tk = vt_ref.shape[3]
    hd = B_HEAD_DIM
    m_scr[...] = jnp.full(m_scr.shape, -jnp.inf, F32)
    acc_scr[...] = jnp.zeros(acc_scr.shape, F32)

    hs = range(heads)

    def logits(j, slot, h):
        start = pl.multiple_of(j * tk, tk)
        s = _dot_nt(ka_ref[h, pl.ds(start, tk), :], qa_ref[h])
        s_scr[slot, h] = s
        smax_scr[slot, h] = jnp.max(s, axis=0, keepdims=True)

    def consume(slot, j, h, masked):
        s = s_scr[slot, h]
        if masked:
            key = lax.broadcasted_iota(jnp.int32, (tk, tq), 0)
            qry = lax.broadcasted_iota(jnp.int32, (tk, tq), 1)
            s = jnp.where(key <= qry, s, -jnp.inf)
            s_max = jnp.max(s, axis=0, keepdims=True)
        else:
            s_max = smax_scr[slot, h]
        m_prev = m_scr[h]
        m_new = jnp.maximum(m_prev, s_max)
        p = jnp.exp2((s - m_new).astype(BF16))
        acc_scr[h] = jnp.exp2(m_prev - m_new) * acc_scr[h] + _dot(vt_ref[h, j], p)
        m_scr[h] = m_new

    def advance(cur, cur_slot, masked=False, nxt=None):
        for h in hs:
            if nxt is not None:
                logits(nxt, 1 - cur_slot, h)
            consume(cur_slot, cur, h, masked)

    for h in hs:
        logits(0, 0, h)

    def body(jj, carry):
        a = 2 * jj
        advance(a, 0, nxt=a + 1)
        advance(a + 1, 1, nxt=a + 2)
        return carry

    lax.fori_loop(0, i // 2, body, 0)

    @pl.when(i % 2 == 0)
    def _():
        advance(i, 0, masked=True)

    @pl.when(i % 2 == 1)
    def _():
        advance(i - 1, 0, nxt=i)
        advance(i, 1, masked=True)

    out_t = [acc_scr[h, 0:hd, :] / acc_scr[h, hd:hd + 1, :] for h in range(heads)]
    o_ref[...] = jnp.concatenate(out_t, axis=0).T


def fox_attention(qa, ka, vt, bn, seq, tq, group):
    heads, n, _ = qa.shape
    rows = vt.shape[2]
    nq = seq // tq
    return pl.pallas_call(
        _fox_kernel,
        grid=(bn, heads // group, nq),
        in_specs=[pl.BlockSpec((group, tq, 128), lambda bi, hi, i: (hi, bi * nq + i, 0)),
                  pl.BlockSpec((group, seq, 128), lambda bi, hi, i: (hi, bi, 0)),
                  pl.BlockSpec((group, nq, rows, tq), lambda bi, hi, i: (hi, bi, 0, 0))],
        out_specs=pl.BlockSpec((tq, group * B_HEAD_DIM), lambda bi, hi, i: (bi * nq + i, hi)),
        out_shape=jax.ShapeDtypeStruct((n, heads * B_HEAD_DIM), F32),
        scratch_shapes=[pltpu.VMEM((group, 1, tq), F32), pltpu.VMEM((group, rows, tq), F32),
                        pltpu.VMEM((2, group, tq, tq), F32), pltpu.VMEM((2, group, 1, tq), F32)],
        compiler_params=_params("parallel", "parallel", "arbitrary"),
        name="fox_attention",
    )(qa, ka, vt)


def _rwkv_prep_kernel(*refs, seq, mix_value):
    (x_ref, gn_ref, w_ref, mu_ref, w0_ref, wup_ref, a0_ref, aup_ref, gup_ref, kk_ref, ka_ref,
     rk_ref, hsum_ref) = refs[:13]
    if mix_value:
        vfirst_ref, v0_ref, vdown_ref, vup_ref = refs[13:17]
    r_out, ld_out, k_out, v_out, kk_out, b_out, g_out, bonus_out, last_scr = refs[-9:]
    tm = x_ref.shape[0]
    cw = r_out.shape[1]
    part = min(tm, PROJECT_ROWS)

    @pl.when((pl.program_id(0) * tm) % seq == 0)
    def _():
        last_scr[...] = jnp.zeros(last_scr.shape, F32)

    def derive(part_idx, pc):
        rows = slice(part_idx * part, (part_idx + 1) * part)
        prev_row = last_scr[...]
        last_scr[...] = pc[part - 1:part, :]
        row = lax.broadcasted_iota(jnp.int32, pc.shape, 0)
        shifted = jnp.where(row == 0, prev_row, pltpu.roll(pc, 1, axis=0))
        pc = pc + (shifted - pc) * mu_ref[...]

        r = pc[:, 0:cw]
        k = pc[:, cw:2 * cw]
        v = pc[:, 2 * cw:3 * cw]
        lora = pc[:, 3 * cw:3 * cw + C_DECAY_LORA + C_AAA_LORA]
        g_lo = pc[:, 3 * cw + C_DECAY_LORA + C_AAA_LORA:]
        if mix_value:
            lam = jax.nn.sigmoid(v0_ref[...] + _mm(_mm(v, vdown_ref[...]), vup_ref[...]))
            v = v + (vfirst_ref[rows, :] - v) * lam
        z = w0_ref[...] + _dot_3pass(jnp.tanh(lora), wup_ref[...])
        log_decay = -DECAY_SCALE * jax.nn.sigmoid(z)
        a = jax.nn.sigmoid(a0_ref[...] + _mm(lora, aup_ref[...]))
        g = _mm(jax.nn.sigmoid(g_lo), gup_ref[...])
        kk = k * kk_ref[...]
        ss = _dot_exact_rhs(kk * kk, hsum_ref[...], 1)
        kk = kk / jnp.maximum(jnp.sqrt(ss), 1e-12)
        k = k * (1.0 + (a - 1.0) * ka_ref[...])
        r_out[rows, :] = r
        ld_out[rows, :] = log_decay
        k_out[rows, :] = k
        v_out[rows, :] = v
        kk_out[rows, :] = kk
        b_out[rows, :] = kk * a
        g_out[rows, :] = g
        bonus_out[rows, :] = _dot_exact_rhs(r * k * rk_ref[...], hsum_ref[...], 1) * v

    _interleaved(tm // part,
                 lambda k: _project(x_ref, gn_ref, w_ref, slice(k * part, (k + 1) * part)), derive)


def rwkv_prep(x, g_norm, w_c, mu, w0, w_up_pad, a0, a_up_pad, g_up, k_k, k_a, r_k, head_sum, seq, tm,
              v_first=None, v0=None, v_down=None, v_up=None):
    n, d = x.shape
    ccols = w_c.shape[0]
    cw = w0.shape[1]
    mix_value = v_first is not None
    const = _const_spec
    in_specs = [pl.BlockSpec((tm, d), lambda i: (i, 0)), const((1, d)), const(w_c.shape),
                const((1, ccols)), const((1, cw)), const(w_up_pad.shape), const((1, cw)),
                const(a_up_pad.shape), const(g_up.shape), const((1, cw)), const((1, cw)),
                const((1, cw)), const(head_sum.shape)]
    args = [x, g_norm, w_c, mu, w0, w_up_pad, a0, a_up_pad, g_up, k_k, k_a, r_k, head_sum]
    if mix_value:
        in_specs += [pl.BlockSpec((tm, cw), lambda i: (i, 0)), const((1, cw)),
                     const(v_down.shape), const(v_up.shape)]
        args += [v_first, v0, v_down, v_up]
    out = jax.ShapeDtypeStruct((n, cw), F32)
    return pl.pallas_call(
        functools.partial(_rwkv_prep_kernel, seq=seq, mix_value=mix_value),
        grid=(n // tm,),
        in_specs=in_specs,
        out_specs=[pl.BlockSpec((tm, cw), lambda i: (i, 0))] * 8,
        out_shape=[out] * 8,
        scratch_shapes=[pltpu.VMEM((1, ccols), F32)],
        compiler_params=_params("arbitrary"),
        name="rwkv_prep",
    )(*args)


def _lane_head(shape):
    return lax.broadcasted_iota(jnp.int32, shape, 1) // C_HEAD_DIM


def _stack_masked(x, pack):
    head = _lane_head(x.shape)
    return jnp.concatenate([jnp.where(head == h, x, 0.0) for h in range(pack)], axis=0)


def _diag_blocks(full, pack):
    n = C_HEAD_DIM
    head = _lane_head((n, pack * n))
    out = full[0:n]
    for h in range(1, pack):
        out = jnp.where(head == h, full[h * n:(h + 1) * n], out)
    return out


def _mm(a, b):
    return _dot(a.astype(BF16), b.astype(BF16))


def _mm_nt(a, b):
    return _dot_nt(a.astype(BF16), b.astype(BF16))


def _mm_tn(a, b):
    return _dot_tn(a.astype(BF16), b.astype(BF16))


def _rwkv_chunk_kernel(r_ref, ld_ref, k_ref, v_ref, kk_ref, b_ref,
                       rq_out, y0_out, g_out, h_out, *, pack):
    tc, cw = r_ref.shape
    c = RWKV_CHUNK
    pw = pack * C_HEAD_DIM
    row = lax.broadcasted_iota(jnp.int32, (c, pw), 0)
    col = lax.broadcasted_iota(jnp.int32, (c, pw), 1) % C_HEAD_DIM
    incl = col <= row
    strict = col < row
    eye = (col == row).astype(F32)
    tril_ones = (lax.broadcasted_iota(jnp.int32, (c, c), 1)
                 <= lax.broadcasted_iota(jnp.int32, (c, c), 0)).astype(BF16)

    units = [(slice(ci * c, (ci + 1) * c), slice(p * pw, (p + 1) * pw))
             for ci in range(tc // c) for p in range(cw // pw)]
    each = lambda fn, *lists: [fn(*args) for args in zip(*lists)]
    stack = lambda x: _stack_masked(x, pack)
    diag = lambda x: _diag_blocks(x, pack)

    cums = {}
    for rows, _ in units:
        if rows.start not in cums:
            cums[rows.start] = _dot_exact_lhs(tril_ones, ld_ref[rows, :], 3)
    cum = [cums[rows.start][:, lanes] for rows, lanes in units]
    ld = [ld_ref[u] for u in units]
    v = [v_ref[u] for u in units]
    p_incl = each(jnp.exp, cum)
    p_inv = each(lambda x: jnp.exp(-x), cum)
    p_prev = each(lambda x, y: jnp.exp(x - y), cum, ld)
    r_t = each(lambda u, s: r_ref[u] * s, units, p_incl)
    kk_t = each(lambda u, s: kk_ref[u] * s, units, p_prev)
    b_t = each(lambda u, s: b_ref[u] * s, units, p_inv)
    k_t = each(lambda u, s: k_ref[u] * s, units, p_inv)

    lhs = each(lambda x, y: jnp.concatenate([x, y], axis=0).astype(BF16), kk_t, r_t)
    rhs_bk = each(lambda x, y: jnp.concatenate([stack(x), stack(y)], axis=0).astype(BF16), b_t, k_t)
    with_bk = each(_dot_nt, lhs, rhs_bk)
    a_b = each(lambda x: jnp.where(strict, x[:c, :pw], 0.0), with_bk)
    a_rb = each(lambda x: jnp.where(incl, x[c:, :pw], 0.0).astype(BF16), with_bk)
    a_kk = each(lambda x: jnp.concatenate([jnp.where(strict, x[:c, pw:], 0.0),
                                           jnp.where(incl, x[c:, pw:], 0.0)], axis=0), with_bk)
    av = each(lambda x, y: _mm(x, stack(y)), a_kk, v)

    lower_left = lambda m: (((row // (2 * m)) == (col // (2 * m)))
                            & ((row // m) % 2 == 1) & ((col // m) % 2 == 0))
    t_inv = each(lambda x: eye - jnp.where(lower_left(1), x, 0.0), a_b)
    m = 2
    while m < c:
        mask = lower_left(m)
        xl = each(lambda x, y: _mm(x, stack(jnp.where(mask, y, 0.0))), t_inv, a_b)
        t_inv = each(lambda x, y: x - _mm(y, stack(x)), t_inv, xl)
        m *= 2

    t_bf = each(lambda x: x.astype(BF16), t_inv)
    side_by_side = lambda x, y: jnp.concatenate([stack(x), stack(y)], axis=1).astype(BF16)
    wu = each(lambda t, x, y: _dot(t, side_by_side(x, y[:c])), t_bf, kk_t, av)
    w = each(lambda x: x[:, :pw], wu)
    u0 = each(lambda x: x[:, pw:], wu)
    for i, u in enumerate(units):
        corr = _dot(a_rb[i], side_by_side(w[i], u0[i]))
        rq_out[u] = r_t[i] - corr[:, :pw]
        y0_out[u] = av[i][c:] - corr[:, pw:]
    for i, u in enumerate(units):
        p_end = p_incl[i][c - 1:c, :]
        wu_b = _mm_tn(wu[i], b_t[i])
        g_out[u] = (eye - diag(wu_b[:pw])) * p_end
        h_out[u] = (diag(_mm_tn(v[i], k_t[i])) - diag(wu_b[pw:])) * p_end


def rwkv_chunk(r, ld, k, v, kk, b, tc):
    n, cw = r.shape
    spec = pl.BlockSpec((tc, cw), lambda i: (i, 0))
    out = jax.ShapeDtypeStruct((n, cw), F32)
    return pl.pallas_call(
        functools.partial(_rwkv_chunk_kernel, pack=RWKV_PACK),
        grid=(n // tc,),
        in_specs=[spec] * 6,
        out_specs=[spec] * 4,
        out_shape=[out] * 4,
        compiler_params=_params("parallel"),
        name="rwkv_chunk",
    )(r, ld, k, v, kk, b)


def _rwkv_state_kernel(rq_ref, y0_ref, g_ref, h_ref, y_out, state_scr, *, pack):
    bn, tc, cw = rq_ref.shape
    c = RWKV_CHUNK
    pw = pack * C_HEAD_DIM
    groups = cw // pw

    @pl.when(pl.program_id(0) == 0)
    def _():
        state_scr[...] = jnp.zeros(state_scr.shape, F32)

    chains = [(bi, p, slice(p * pw, (p + 1) * pw)) for bi in range(bn) for p in range(groups)]
    state = [state_scr[bi, p] for bi, p, _ in chains]
    for ci in range(tc // c):
        rows = slice(ci * c, (ci + 1) * c)
        for i, (bi, p, lanes) in enumerate(chains):
            y_out[bi, rows, lanes] = (_mm_nt(rq_ref[bi, rows, lanes], _stack_masked(state[i], pack))
                                      + y0_ref[bi, rows, lanes])
        for i, (bi, p, lanes) in enumerate(chains):
            state[i] = _mm(state[i], _stack_masked(g_ref[bi, rows, lanes], pack)) + h_ref[bi, rows, lanes]
    for i, (bi, p, _) in enumerate(chains):
        state_scr[bi, p] = state[i]


def rwkv_state(rq, y0, g, h, tc):
    bn, s, cw = rq.shape
    pw = RWKV_PACK * C_HEAD_DIM
    spec = pl.BlockSpec((bn, tc, cw), lambda t: (0, t, 0))
    return pl.pallas_call(
        functools.partial(_rwkv_state_kernel, pack=RWKV_PACK),
        grid=(s // tc,),
        in_specs=[spec] * 4,
        out_specs=spec,
        out_shape=jax.ShapeDtypeStruct(rq.shape, F32),
        scratch_shapes=[pltpu.VMEM((bn, cw // pw, C_HEAD_DIM, pw), F32)],
        compiler_params=_params("arbitrary"),
        name="rwkv_state",
    )(rq, y0, g, h)


def _merge_kernel(x_ref, gn_ref, wg_ref, ya_ref, yb_ref, yc_ref, bonus_ref, g_ref, gb_ref, lng_ref, lnb_ref,
                  hsum_ref, pa_ref, pb_ref, pc_ref, wo_ref, o_ref):
    d = x_ref.shape[1]
    x = x_ref[...]
    h = _rmsnorm(x, gn_ref[...]).astype(BF16)
    y = yc_ref[...]
    inv_n = 1.0 / C_HEAD_DIM
    mu = _dot_exact_rhs(y, hsum_ref[...], 1) * inv_n
    yc = y - mu
    var = _dot_exact_rhs(yc * yc, hsum_ref[...], 1) * inv_n
    yc = yc * lax.rsqrt(var + GN_EPS) * lng_ref[...] + lnb_ref[...]
    yc = (yc + bonus_ref[...]) * g_ref[...]
    branches = ((ya_ref, pa_ref), (yb_ref, pb_ref), (yc, pc_ref))
    merged = None
    for bi, (y_b, p_ref) in enumerate(branches):
        y_b = y_b if bi == 2 else y_b[...]
        gate = jax.nn.sigmoid(_dot_nt(h, wg_ref[bi * d:(bi + 1) * d, :]) + gb_ref[bi:bi + 1, :])
        term = gate * _dot(y_b.astype(BF16), p_ref[...])
        merged = term if merged is None else merged + term
    o_ref[...] = x + _dot(merged.astype(BF16), wo_ref[...])


def merge_out(x, g_norm, w_g, ya, yb, yc, bonus, g, gate_bias, lnx_g, lnx_b, head_sum,
              p_a, p_b, p_c, w_out, tm):
    n, d = x.shape
    tile = lambda a: pl.BlockSpec((tm, a.shape[1]), lambda i: (i, 0))
    const = lambda a: _const_spec(a.shape)
    return pl.pallas_call(
        _merge_kernel,
        grid=(n // tm,),
        in_specs=[tile(x), const(g_norm), const(w_g), tile(ya), tile(yb), tile(yc), tile(bonus), tile(g),
                  const(gate_bias), const(lnx_g), const(lnx_b), const(head_sum),
                  const(p_a), const(p_b), const(p_c), const(w_out)],
        out_specs=pl.BlockSpec((tm, d), lambda i: (i, 0)),
        out_shape=jax.ShapeDtypeStruct((n, d), F32),
        compiler_params=_params("parallel"),
        name="merge_out",
    )(x, g_norm, w_g, ya, yb, yc, bonus, g, gate_bias, lnx_g, lnx_b, head_sum, p_a, p_b, p_c, w_out)


def _ffn_kernel(x_ref, g_ref, wg_ref, wu_ref, wd_ref, gf_ref, o_ref, h_scr, acc_scr, *, final_norm):
    j = pl.program_id(1)

    @pl.when(j == 0)
    def _():
        x = x_ref[...]
        h_scr[...] = _rmsnorm(x, g_ref[...]).astype(BF16)
        acc_scr[...] = x

    h = h_scr[...]
    act = jax.nn.silu(_dot(h, wg_ref[...])) * _dot(h, wu_ref[...])
    acc_scr[...] += _dot(act.astype(BF16), wd_ref[...])

    @pl.when(j == pl.num_programs(1) - 1)
    def _():
        y = acc_scr[...]
        o_ref[...] = _rmsnorm(y, gf_ref[...]) if final_norm else y


def ffn(x, g, w_gate_up, w_down, g_final, tm, tf, final_norm):
    n, d = x.shape
    dff = w_down.shape[0]
    nf = dff // tf
    return pl.pallas_call(
        functools.partial(_ffn_kernel, final_norm=final_norm),
        grid=(n // tm, nf),
        in_specs=[pl.BlockSpec((tm, d), lambda i, j: (i, 0)),
                  pl.BlockSpec((1, d), lambda i, j: (0, 0)),
                  pl.BlockSpec((d, tf), lambda i, j: (0, j)),
                  pl.BlockSpec((d, tf), lambda i, j: (0, j + nf)),
                  pl.BlockSpec((tf, d), lambda i, j: (j, 0)),
                  pl.BlockSpec((1, d), lambda i, j: (0, 0))],
        out_specs=pl.BlockSpec((tm, d), lambda i, j: (i, 0)),
        out_shape=jax.ShapeDtypeStruct((n, d), F32),
        scratch_shapes=[pltpu.VMEM((tm, d), BF16), pltpu.VMEM((tm, d), F32)],
        compiler_params=_params("parallel", "arbitrary"),
        name="ffn",
    )(x, g, w_gate_up, w_gate_up, w_down, g_final)


def _pad_rows(w, mult):
    pad = (-w.shape[0]) % mult
    return jnp.pad(w, ((0, pad), (0, 0))) if pad else w


def _largest_tile(total, cap):
    t = cap
    while total % t:
        t //= 2
    return t


def kernel(x, norm_mix, w_in, gate_bias, a_ln_g, a_ln_b, a_w_s, a_b_s, b_f_bias, c_mu, c_w0, c_w_up, c_a0, c_a_up, c_g_up, c_k_k, c_k_a, c_r_k, c_lnx_g, c_lnx_b, c_v0, c_v_down, c_v_up, p_a, p_b, p_c, w_out, norm_ffn, w_gate_up, w_down, norm_final):
    bn, s, d = x.shape
    depth = w_in.shape[0]
    n = bn * s
    a_width = a_ln_g.shape[1]
    b_width = B_HEADS * B_HEAD_DIM
    c_width = C_HEADS * C_HEAD_DIM
    a_cols = 2 * a_width
    b_cols = 3 * b_width + B_HEADS
    c_cols = 3 * c_width + C_DECAY_LORA + C_AAA_LORA + C_GATE_LORA
    dff = w_down.shape[1]
    assert s % 128 == 0 and dff % 256 == 0

    tm = _largest_tile(n, 512)
    t_attn = _largest_tile(s, 512)
    t_chunk = _largest_tile(s, 256)
    t_state = _largest_tile(s, 256)
    tf = dff // 2 if (dff // 2) % 128 == 0 else dff

    head_sum = jnp.kron(jnp.eye(C_HEADS, dtype=BF16), jnp.ones((C_HEAD_DIM, C_HEAD_DIM), BF16))
    row2 = lambda p: p.reshape(1, -1)

    xf = x.reshape(n, d)
    w_in_t = jnp.swapaxes(w_in, 1, 2)
    v_first = None
    for l in range(depth):
        w_a = w_in_t[l, :a_cols].astype(BF16)
        w_b = _pad_rows(w_in_t[l, a_cols:a_cols + b_cols].astype(BF16), 128)
        w_c = w_in_t[l, a_cols + b_cols:a_cols + b_cols + c_cols].astype(BF16)
        w_g = w_in_t[l, a_cols + b_cols + c_cols:].astype(BF16)
        g_mix = row2(norm_mix[l])

        b_s_full = jnp.repeat(a_b_s[l].T, a_width // A_GROUPS, axis=1)
        ya = gmlp(xf, g_mix, w_a, row2(a_ln_g[l]), row2(a_ln_b[l]), a_w_s[l], b_s_full, tm)

        f_bias_row = jnp.pad(b_f_bias[l], (0, 128 - B_HEADS)).reshape(1, 128)
        qa, ka, vt = fox_prep(xf, g_mix, w_b, f_bias_row, s, tm)
        yb = fox_attention(qa, ka, vt, bn, s, tm, FOX_HEAD_GROUP)

        zeros_lora = jnp.zeros((C_AAA_LORA, c_width), F32)
        w_up_pad = jnp.concatenate([c_w_up[l], zeros_lora], axis=0)
        a_up_pad = jnp.concatenate([jnp.zeros((C_DECAY_LORA, c_width), F32), c_a_up[l]], axis=0)
        mix = {} if l == 0 else dict(v_first=v_first, v0=row2(c_v0[l - 1]),
                                     v_down=c_v_down[l - 1], v_up=c_v_up[l - 1])
        r_c, ld_c, k_c, v_c, kk_c, b_c, g_c, bonus_c = rwkv_prep(
            xf, g_mix, w_c, row2(c_mu[l]), row2(c_w0[l]), w_up_pad, row2(c_a0[l]), a_up_pad, c_g_up[l],
            row2(c_k_k[l]), row2(c_k_a[l]), row2(c_r_k[l]), head_sum, s, tm, **mix)
        if l == 0:
            v_first = v_c
        rq, y0, g_mat, h_mat = rwkv_chunk(r_c, ld_c, k_c, v_c, kk_c, b_c, t_chunk)
        per_batch = lambda t: t.reshape(bn, s, c_width)
        yc = rwkv_state(per_batch(rq), per_batch(y0), per_batch(g_mat), per_batch(h_mat), t_state)
        yc = yc.reshape(n, c_width)

        xf = merge_out(xf, g_mix, w_g, ya, yb, yc, bonus_c, g_c, gate_bias[l], row2(c_lnx_g[l]),
                       row2(c_lnx_b[l]), head_sum, p_a[l].astype(BF16), p_b[l].astype(BF16),
                       p_c[l].astype(BF16), w_out[l].astype(BF16), tm)
        xf = ffn(xf, row2(norm_ffn[l]), w_gate_up[l].astype(BF16), w_down[l].astype(BF16),
                 row2(norm_final), _largest_tile(n, 1024), tf, final_norm=(l == depth - 1))
    return xf.reshape(bn, s, d)
```

```python
import functools

import jax
import jax.numpy as jnp
from jax import lax
from jax.experimental import pallas as pl
from jax.experimental.pallas import tpu as pltpu

F32 = jnp.float32
BF16 = jnp.bfloat16

V7X_VMEM_LIMIT_BYTES = 56 * 1024 * 1024

NORM_EPS = 1e-6
LN_EPS = 1e-5
GN_EPS = 64e-5

N_BRANCH = 3
A_GROUPS = 4
A_CHUNK = 128
B_HEADS = 8
B_HEAD_DIM = 64
C_HEADS = 8
C_HEAD_DIM = 64
C_DECAY_LORA = 64
C_AAA_LORA = 64
C_GATE_LORA = 128
PROJECT_ROWS = 256
LOG2_E = 1.4426950408889634
DECAY_SCALE = 0.6065306597126334
FOX_HEAD_GROUP = 4
FOX_V_ROWS = 80
RWKV_CHUNK = 64
RWKV_PACK = 2


def _params(*semantics):
    return pltpu.CompilerParams(dimension_semantics=semantics,
                                vmem_limit_bytes=V7X_VMEM_LIMIT_BYTES)


def _rmsnorm(x, g):
    return x * lax.rsqrt(jnp.mean(x * x, axis=-1, keepdims=True) + NORM_EPS) * g


def _dot(a, b, precision=None):
    return jnp.dot(a, b, preferred_element_type=F32, precision=precision)


def _dot_nt(a, b, precision=None):
    return lax.dot_general(a, b, (((1,), (1,)), ((), ())),
                           preferred_element_type=F32, precision=precision)


def _bf16_terms(x, n):
    terms = []
    for _ in range(n - 1):
        t = x.astype(BF16)
        terms.append(t)
        x = x - t.astype(F32)
    terms.append(x.astype(BF16))
    return terms


def _dot_exact_rhs(x, m, n_terms):
    return sum(_dot(t, m) for t in _bf16_terms(x, n_terms))


def _dot_exact_lhs(m, x, n_terms):
    return sum(_dot(m, t) for t in _bf16_terms(x, n_terms))


def _dot_3pass(a, b):
    a_hi, a_lo = _bf16_terms(a, 2)
    b_hi, b_lo = _bf16_terms(b, 2)
    return _dot(a_hi, b_hi) + (_dot(a_lo, b_hi) + _dot(a_hi, b_lo))


def _dot_tn(a, b, precision=None):
    return lax.dot_general(a, b, (((0,), (0,)), ((), ())),
                           preferred_element_type=F32, precision=precision)


def _const_spec(shape):
    return pl.BlockSpec(shape, lambda *_: (0,) * len(shape), pipeline_mode=pl.Buffered(1))


def _project(x_ref, g_ref, wt_ref, rows=slice(None)):
    return _dot_nt(_rmsnorm(x_ref[rows, :], g_ref[...]).astype(BF16), wt_ref[...])


def _interleaved(n_parts, produce, consume):
    ready = produce(0)
    for k in range(n_parts):
        upcoming = produce(k + 1) if k + 1 < n_parts else None
        consume(k, ready)
        ready = upcoming


def _gmlp_kernel(x_ref, gn_ref, w_ref, lng_ref, lnb_ref, ws_ref, bs_ref, o_ref, *, width):
    tm = x_ref.shape[0]
    gd = width // A_GROUPS
    part = min(tm, PROJECT_ROWS)
    row = lax.broadcasted_iota(jnp.int32, (A_CHUNK, A_CHUNK), 0)
    col = lax.broadcasted_iota(jnp.int32, (A_CHUNK, A_CHUNK), 1)
    causal = col <= row

    def mix(k, pa):
        for c in range(part // A_CHUNK):
            rows = slice(c * A_CHUNK, (c + 1) * A_CHUNK)
            out_rows = slice(k * part + c * A_CHUNK, k * part + (c + 1) * A_CHUNK)
            act = jax.nn.gelu(pa[rows, :])
            u = act[:, :width]
            v = act[:, width:]
            mu = jnp.mean(v, axis=-1, keepdims=True)
            var = jnp.mean(jnp.square(v - mu), axis=-1, keepdims=True)
            v = (v - mu) * lax.rsqrt(var + LN_EPS) * lng_ref[...] + lnb_ref[...]
            v = v.astype(BF16)
            for g in range(A_GROUPS):
                cols = slice(g * gd, (g + 1) * gd)
                w = jnp.where(causal, ws_ref[g], 0.0).astype(BF16)
                mixed = _dot(w, v[:, cols]) + bs_ref[:, cols]
                o_ref[out_rows, cols] = u[:, cols] * mixed

    _interleaved(tm // part,
                 lambda k: _project(x_ref, gn_ref, w_ref, slice(k * part, (k + 1) * part)), mix)


def gmlp(x, g_norm, w_a, ln_g, ln_b, w_s, b_s_full, tm):
    n, d = x.shape
    width = w_a.shape[0] // 2
    return pl.pallas_call(
        functools.partial(_gmlp_kernel, width=width),
        grid=(n // tm,),
        in_specs=[pl.BlockSpec((tm, d), lambda i: (i, 0)),
                  _const_spec((1, d)), _const_spec(w_a.shape),
                  _const_spec((1, width)), _const_spec((1, width)),
                  _const_spec((A_GROUPS, A_CHUNK, A_CHUNK)), _const_spec((A_CHUNK, width))],
        out_specs=pl.BlockSpec((tm, width), lambda i: (i, 0)),
        out_shape=jax.ShapeDtypeStruct((n, width), F32),
        compiler_params=_params("parallel"),
        name="gmlp",
    )(x, g_norm, w_a, ln_g, ln_b, w_s, b_s_full)


def _one_hot(shape, hit):
    r = lax.broadcasted_iota(jnp.int32, shape, 0)
    c = lax.broadcasted_iota(jnp.int32, shape, 1)
    return jnp.where(hit(r, c), 1.0, 0.0).astype(BF16)


def _fox_prep_kernel(x_ref, gn_ref, w_ref, fb_ref, qa_ref, ka_ref, vt_ref, carry_scr, *, seq, scale):
    tm = x_ref.shape[0]
    hd = B_HEAD_DIM
    width = B_HEADS * hd
    part = min(tm, PROJECT_ROWS)

    @pl.when((pl.program_id(0) * tm) % seq == 0)
    def _():
        carry_scr[...] = jnp.zeros(carry_scr.shape, F32)

    tril = (lax.broadcasted_iota(jnp.int32, (part, part), 1)
            <= lax.broadcasted_iota(jnp.int32, (part, part), 0)).astype(BF16)
    lane = lax.broadcasted_iota(jnp.int32, (part, 128), 1)
    low = lane < hd
    ones_lanes = (lane >= hd) & (lane < hd + 3)
    v_rows_n = vt_ref.shape[2]
    row_t = lax.broadcasted_iota(jnp.int32, (v_rows_n, part), 0)
    place_c = _one_hot((128, B_HEADS * 128),
                       lambda r, c: (r < 3 * B_HEADS) & (c == 128 * (r % B_HEADS) + hd + r // B_HEADS))

    def derive(part_idx, x):
        rows = slice(part_idx * part, (part_idx + 1) * part)
        log_f = jax.nn.log_sigmoid(x[:, 3 * width:3 * width + 128] + fb_ref[...]) * LOG2_E
        cum = _dot_exact_lhs(tril, log_f, 3) + carry_scr[...]
        carry_scr[...] = cum[part - 1:part, :]
        terms = _bf16_terms(jnp.where(lane < B_HEADS, -cum, 0.0), 3)
        c_terms = sum(pltpu.roll(t.astype(F32), B_HEADS * i, axis=1) if i else t.astype(F32)
                      for i, t in enumerate(terms)).astype(BF16)
        c_placed = _dot(c_terms, place_c)
        for pair in range(B_HEADS // 2):
            lanes = slice(pair * 128, pair * 128 + 128)
            q = x[:, lanes] * (scale * LOG2_E)
            k = x[:, width:2 * width][:, lanes]
            v = x[:, 2 * width:3 * width][:, lanes].astype(BF16)
            for odd in range(2):
                h = 2 * pair + odd
                q_h, k_h = (pltpu.roll(q, hd, axis=1), pltpu.roll(k, hd, axis=1)) if odd else (q, k)
                v_rows = _one_hot((v_rows_n, 128), lambda r, c: (r < hd) & (c == r + odd * hd))
                qa_ref[h, rows, :] = jnp.where(ones_lanes, 1.0, jnp.where(low, q_h, 0.0)).astype(BF16)
                ka_ref[h, rows, :] = (jnp.where(low, k_h, 0.0)
                                      + c_placed[:, h * 128:(h + 1) * 128]).astype(BF16)
                vt_ref[h, 0, :, rows] = jnp.where(row_t == hd, 1.0, _dot_nt(v_rows, v)).astype(BF16)

    _interleaved(tm // part,
                 lambda k: _project(x_ref, gn_ref, w_ref, slice(k * part, (k + 1) * part)), derive)


def fox_prep(x, g_norm, w_b, f_bias_row, seq, tm):
    n, d = x.shape
    return pl.pallas_call(
        functools.partial(_fox_prep_kernel, seq=seq, scale=B_HEAD_DIM ** -0.5),
        grid=(n // tm,),
        in_specs=[pl.BlockSpec((tm, d), lambda i: (i, 0)),
                  _const_spec((1, d)), _const_spec(w_b.shape), _const_spec((1, 128))],
        out_specs=[pl.BlockSpec((B_HEADS, tm, 128), lambda i: (0, i, 0)),
                   pl.BlockSpec((B_HEADS, tm, 128), lambda i: (0, i, 0)),
                   pl.BlockSpec((B_HEADS, 1, FOX_V_ROWS, tm), lambda i: (0, i, 0, 0))],
        out_shape=[jax.ShapeDtypeStruct((B_HEADS, n, 128), BF16),
                   jax.ShapeDtypeStruct((B_HEADS, n, 128), BF16),
                   jax.ShapeDtypeStruct((B_HEADS, n // tm, FOX_V_ROWS, tm), BF16)],
        scratch_shapes=[pltpu.VMEM((1, 128), F32)],
        compiler_params=_params("arbitrary"),
        name="fox_prep",
    )(x, g_norm, w_b, f_bias_row)


def _fox_kernel(qa_ref, ka_ref, vt_ref, o_ref, m_scr, acc_scr, s_scr, smax_scr):
    i = pl.program_id(2)
    heads, tq, _ = qa_ref.shape
    tk = vt_ref.shape[3]
    hd = B_HEAD_DIM
    m_scr[...] = jnp.full(m_scr.shape, -jnp.inf, F32)
    acc_scr[...] = jnp.zeros(acc_scr.shape, F32)

    hs = range(heads)

    def logits(j, slot, h):
        start = pl.multiple_of(j * tk, tk)
        s = _dot_nt(ka_ref[h, pl.ds(start, tk), :], qa_ref[h])
        s_scr[slot, h] = s
        smax_scr[slot, h] = jnp.max(s, axis=0, keepdims=True)

    def consume(slot, j, h, masked):
        s = s_scr[slot, h]
        if masked:
            key = lax.broadcasted_iota(jnp.int32, (tk, tq), 0)
            qry = lax.broadcasted_iota(jnp.int32, (tk, tq), 1)
            s = jnp.where(key <= qry, s, -jnp.inf)
            s_max = jnp.max(s, axis=0, keepdims=True)
        else:
            s_max = smax_scr[slot, h]
        m_prev = m_scr[h]
        m_new = jnp.maximum(m_prev, s_max)
        p = jnp.exp2((s - m_new).astype(BF16))
        acc_scr[h] = jnp.exp2(m_prev - m_new) * acc_scr[h] + _dot(vt_ref[h, j], p)
        m_scr[h] = m_new

    def advance(cur, cur_slot, masked=False, nxt=None):
        for h in hs:
            if nxt is not None:
                logits(nxt, 1 - cur_slot, h)
            consume(cur_slot, cur, h, masked)

    for h in hs:
        logits(0, 0, h)

    def body(jj, carry):
        a = 2 * jj
        advance(a, 0, nxt=a + 1)
        advance(a + 1, 1, nxt=a + 2)
        return carry

    lax.fori_loop(0, i // 2, body, 0)

    @pl.when(i % 2 == 0)
    def _():
        advance(i, 0, masked=True)

    @pl.when(i % 2 == 1)
    def _():
        advance(i - 1, 0, nxt=i)
        advance(i, 1, masked=True)

    out_t = [acc_scr[h, 0:hd, :] / acc_scr[h, hd:hd + 1, :] for h in range(heads)]
    o_ref[...] = jnp.concatenate(out_t, axis=0).T


def fox_attention(qa, ka, vt, bn, seq, tq, group):
    heads, n, _ = qa.shape
    rows = vt.shape[2]
    nq = seq // tq
    return pl.pallas_call(
        _fox_kernel,
        grid=(bn, heads // group, nq),
        in_specs=[pl.BlockSpec((group, tq, 128), lambda bi, hi, i: (hi, bi * nq + i, 0)),
                  pl.BlockSpec((group, seq, 128), lambda bi, hi, i: (hi, bi, 0)),
                  pl.BlockSpec((group, nq, rows, tq), lambda bi, hi, i: (hi, bi, 0, 0))],
        out_specs=pl.BlockSpec((tq, group * B_HEAD_DIM), lambda bi, hi, i: (bi * nq + i, hi)),
        out_shape=jax.ShapeDtypeStruct((n, heads * B_HEAD_DIM), F32),
        scratch_shapes=[pltpu.VMEM((group, 1, tq), F32), pltpu.VMEM((group, rows, tq), F32),
                        pltpu.VMEM((2, group, tq, tq), F32), pltpu.VMEM((2, group, 1, tq), F32)],
        compiler_params=_params("parallel", "parallel", "arbitrary"),
        name="fox_attention",
    )(qa, ka, vt)


def _rwkv_prep_kernel(*refs, seq, mix_value):
    (x_ref, gn_ref, w_ref, mu_ref, w0_ref, wup_ref, a0_ref, aup_ref, gup_ref, kk_ref, ka_ref,
     rk_ref, hsum_ref) = refs[:13]
    if mix_value:
        vfirst_ref, v0_ref, vdown_ref, vup_ref = refs[13:17]
    r_out, ld_out, k_out, v_out, kk_out, b_out, g_out, bonus_out, last_scr = refs[-9:]
    tm = x_ref.shape[0]
    cw = r_out.shape[1]
    part = min(tm, PROJECT_ROWS)

    @pl.when((pl.program_id(0) * tm) % seq == 0)
    def _():
        last_scr[...] = jnp.zeros(last_scr.shape, F32)

    def derive(part_idx, pc):
        rows = slice(part_idx * part, (part_idx + 1) * part)
        prev_row = last_scr[...]
        last_scr[...] = pc[part - 1:part, :]
        row = lax.broadcasted_iota(jnp.int32, pc.shape, 0)
        shifted = jnp.where(row == 0, prev_row, pltpu.roll(pc, 1, axis=0))
        pc = pc + (shifted - pc) * mu_ref[...]

        r = pc[:, 0:cw]
        k = pc[:, cw:2 * cw]
        v = pc[:, 2 * cw:3 * cw]
        lora = pc[:, 3 * cw:3 * cw + C_DECAY_LORA + C_AAA_LORA]
        g_lo = pc[:, 3 * cw + C_DECAY_LORA + C_AAA_LORA:]
        if mix_value:
            lam = jax.nn.sigmoid(v0_ref[...] + _mm(_mm(v, vdown_ref[...]), vup_ref[...]))
            v = v + (vfirst_ref[rows, :] - v) * lam
        z = w0_ref[...] + _dot_3pass(jnp.tanh(lora), wup_ref[...])
        log_decay = -DECAY_SCALE * jax.nn.sigmoid(z)
        a = jax.nn.sigmoid(a0_ref[...] + _mm(lora, aup_ref[...]))
        g = _mm(jax.nn.sigmoid(g_lo), gup_ref[...])
        kk = k * kk_ref[...]
        ss = _dot_exact_rhs(kk * kk, hsum_ref[...], 1)
        kk = kk / jnp.maximum(jnp.sqrt(ss), 1e-12)
        k = k * (1.0 + (a - 1.0) * ka_ref[...])
        r_out[rows, :] = r
        ld_out[rows, :] = log_decay
        k_out[rows, :] = k
        v_out[rows, :] = v
        kk_out[rows, :] = kk
        b_out[rows, :] = kk * a
        g_out[rows, :] = g
        bonus_out[rows, :] = _dot_exact_rhs(r * k * rk_ref[...], hsum_ref[...], 1) * v

    _interleaved(tm // part,
                 lambda k: _project(x_ref, gn_ref, w_ref, slice(k * part, (k + 1) * part)), derive)


def rwkv_prep(x, g_norm, w_c, mu, w0, w_up_pad, a0, a_up_pad, g_up, k_k, k_a, r_k, head_sum, seq, tm,
              v_first=None, v0=None, v_down=None, v_up=None):
    n, d = x.shape
    ccols = w_c.shape[0]
    cw = w0.shape[1]
    mix_value = v_first is not None
    const = _const_spec
    in_specs = [pl.BlockSpec((tm, d), lambda i: (i, 0)), const((1, d)), const(w_c.shape),
                const((1, ccols)), const((1, cw)), const(w_up_pad.shape), const((1, cw)),
                const(a_up_pad.shape), const(g_up.shape), const((1, cw)), const((1, cw)),
                const((1, cw)), const(head_sum.shape)]
    args = [x, g_norm, w_c, mu, w0, w_up_pad, a0, a_up_pad, g_up, k_k, k_a, r_k, head_sum]
    if mix_value:
        in_specs += [pl.BlockSpec((tm, cw), lambda i: (i, 0)), const((1, cw)),
                     const(v_down.shape), const(v_up.shape)]
        args += [v_first, v0, v_down, v_up]
    out = jax.ShapeDtypeStruct((n, cw), F32)
    return pl.pallas_call(
        functools.partial(_rwkv_prep_kernel, seq=seq, mix_value=mix_value),
        grid=(n // tm,),
        in_specs=in_specs,
        out_specs=[pl.BlockSpec((tm, cw), lambda i: (i, 0))] * 8,
        out_shape=[out] * 8,
        scratch_shapes=[pltpu.VMEM((1, ccols), F32)],
        compiler_params=_params("arbitrary"),
        name="rwkv_prep",
    )(*args)


def _lane_head(shape):
    return lax.broadcasted_iota(jnp.int32, shape, 1) // C_HEAD_DIM


def _stack_masked(x, pack):
    head = _lane_head(x.shape)
    return jnp.concatenate([jnp.where(head == h, x, 0.0) for h in range(pack)], axis=0)


def _diag_blocks(full, pack):
    n = C_HEAD_DIM
    head = _lane_head((n, pack * n))
    out = full[0:n]
    for h in range(1, pack):
        out = jnp.where(head == h, full[h * n:(h + 1) * n], out)
    return out


def _mm(a, b):
    return _dot(a.astype(BF16), b.astype(BF16))


def _mm_nt(a, b):
    return _dot_nt(a.astype(BF16), b.astype(BF16))


def _mm_tn(a, b):
    return _dot_tn(a.astype(BF16), b.astype(BF16))


def _rwkv_chunk_kernel(r_ref, ld_ref, k_ref, v_ref, kk_ref, b_ref,
                       rq_out, y0_out, g_out, h_out, *, pack):
    tc, cw = r_ref.shape
    c = RWKV_CHUNK
    pw = pack * C_HEAD_DIM
    row = lax.broadcasted_iota(jnp.int32, (c, pw), 0)
    col = lax.broadcasted_iota(jnp.int32, (c, pw), 1) % C_HEAD_DIM
    incl = col <= row
    strict = col < row
    eye = (col == row).astype(F32)
    tril_ones = (lax.broadcasted_iota(jnp.int32, (c, c), 1)
                 <= lax.broadcasted_iota(jnp.int32, (c, c), 0)).astype(BF16)

    units = [(slice(ci * c, (ci + 1) * c), slice(p * pw, (p + 1) * pw))
             for ci in range(tc // c) for p in range(cw // pw)]
    each = lambda fn, *lists: [fn(*args) for args in zip(*lists)]
    stack = lambda x: _stack_masked(x, pack)
    diag = lambda x: _diag_blocks(x, pack)

    cums = {}
    for rows, _ in units:
        if rows.start not in cums:
            cums[rows.start] = _dot_exact_lhs(tril_ones, ld_ref[rows, :], 3)
    cum = [cums[rows.start][:, lanes] for rows, lanes in units]
    ld = [ld_ref[u] for u in units]
    v = [v_ref[u] for u in units]
    p_incl = each(jnp.exp, cum)
    p_inv = each(lambda x: jnp.exp(-x), cum)
    p_prev = each(lambda x, y: jnp.exp(x - y), cum, ld)
    r_t = each(lambda u, s: r_ref[u] * s, units, p_incl)
    kk_t = each(lambda u, s: kk_ref[u] * s, units, p_prev)
    b_t = each(lambda u, s: b_ref[u] * s, units, p_inv)
    k_t = each(lambda u, s: k_ref[u] * s, units, p_inv)

    lhs = each(lambda x, y: jnp.concatenate([x, y], axis=0).astype(BF16), kk_t, r_t)
    rhs_bk = each(lambda x, y: jnp.concatenate([stack(x), stack(y)], axis=0).astype(BF16), b_t, k_t)
    with_bk = each(_dot_nt, lhs, rhs_bk)
    a_b = each(lambda x: jnp.where(strict, x[:c, :pw], 0.0), with_bk)
    a_rb = each(lambda x: jnp.where(incl, x[c:, :pw], 0.0).astype(BF16), with_bk)
    a_kk = each(lambda x: jnp.concatenate([jnp.where(strict, x[:c, pw:], 0.0),
                                           jnp.where(incl, x[c:, pw:], 0.0)], axis=0), with_bk)
    av = each(lambda x, y: _mm(x, stack(y)), a_kk, v)

    lower_left = lambda m: (((row // (2 * m)) == (col // (2 * m)))
                            & ((row // m) % 2 == 1) & ((col // m) % 2 == 0))
    t_inv = each(lambda x: eye - jnp.where(lower_left(1), x, 0.0), a_b)
    m = 2
    while m < c:
        mask = lower_left(m)
        xl = each(lambda x, y: _mm(x, stack(jnp.where(mask, y, 0.0))), t_inv, a_b)
        t_inv = each(lambda x, y: x - _mm(y, stack(x)), t_inv, xl)
        m *= 2

    t_bf = each(lambda x: x.astype(BF16), t_inv)
    side_by_side = lambda x, y: jnp.concatenate([stack(x), stack(y)], axis=1).astype(BF16)
    wu = each(lambda t, x, y: _dot(t, side_by_side(x, y[:c])), t_bf, kk_t, av)
    w = each(lambda x: x[:, :pw], wu)
    u0 = each(lambda x: x[:, pw:], wu)
    for i, u in enumerate(units):
        corr = _dot(a_rb[i], side_by_side(w[i], u0[i]))
        rq_out[u] = r_t[i] - corr[:, :pw]
        y0_out[u] = av[i][c:] - corr[:, pw:]
    for i, u in enumerate(units):
        p_end = p_incl[i][c - 1:c, :]
        wu_b = _mm_tn(wu[i], b_t[i])
        g_out[u] = (eye - diag(wu_b[:pw])) * p_end
        h_out[u] = (diag(_mm_tn(v[i], k_t[i])) - diag(wu_b[pw:])) * p_end


def rwkv_chunk(r, ld, k, v, kk, b, tc):
    n, cw = r.shape
    spec = pl.BlockSpec((tc, cw), lambda i: (i, 0))
    out = jax.ShapeDtypeStruct((n, cw), F32)
    return pl.pallas_call(
        functools.partial(_rwkv_chunk_kernel, pack=RWKV_PACK),
        grid=(n // tc,),
        in_specs=[spec] * 6,
        out_specs=[spec] * 4,
        out_shape=[out] * 4,
        compiler_params=_params("parallel"),
        name="rwkv_chunk",
    )(r, ld, k, v, kk, b)


def _rwkv_state_kernel(rq_ref, y0_ref, g_ref, h_ref, y_out, state_scr, *, pack):
    bn, tc, cw = rq_ref.shape
    c = RWKV_CHUNK
    pw = pack * C_HEAD_DIM
    groups = cw // pw

    @pl.when(pl.program_id(0) == 0)
    def _():
        state_scr[...] = jnp.zeros(state_scr.shape, F32)

    chains = [(bi, p, slice(p * pw, (p + 1) * pw)) for bi in range(bn) for p in range(groups)]
    state = [state_scr[bi, p] for bi, p, _ in chains]
    for ci in range(tc // c):
        rows = slice(ci * c, (ci + 1) * c)
        for i, (bi, p, lanes) in enumerate(chains):
            y_out[bi, rows, lanes] = (_mm_nt(rq_ref[bi, rows, lanes], _stack_masked(state[i], pack))
                                      + y0_ref[bi, rows, lanes])
        for i, (bi, p, lanes) in enumerate(chains):
            state[i] = _mm(state[i], _stack_masked(g_ref[bi, rows, lanes], pack)) + h_ref[bi, rows, lanes]
    for i, (bi, p, _) in enumerate(chains):
        state_scr[bi, p] = state[i]


def rwkv_state(rq, y0, g, h, tc):
    bn, s, cw = rq.shape
    pw = RWKV_PACK * C_HEAD_DIM
    spec = pl.BlockSpec((bn, tc, cw), lambda t: (0, t, 0))
    return pl.pallas_call(
        functools.partial(_rwkv_state_kernel, pack=RWKV_PACK),
        grid=(s // tc,),
        in_specs=[spec] * 4,
        out_specs=spec,
        out_shape=jax.ShapeDtypeStruct(rq.shape, F32),
        scratch_shapes=[pltpu.VMEM((bn, cw // pw, C_HEAD_DIM, pw), F32)],
        compiler_params=_params("arbitrary"),
        name="rwkv_state",
    )(rq, y0, g, h)


def _merge_kernel(x_ref, gn_ref, wg_ref, ya_ref, yb_ref, yc_ref, bonus_ref, g_ref, gb_ref, lng_ref, lnb_ref,
                  hsum_ref, pa_ref, pb_ref, pc_ref, wo_ref, o_ref):
    d = x_ref.shape[1]
    x = x_ref[...]
    h = _rmsnorm(x, gn_ref[...]).astype(BF16)
    y = yc_ref[...]
    inv_n = 1.0 / C_HEAD_DIM
    mu = _dot_exact_rhs(y, hsum_ref[...], 1) * inv_n
    yc = y - mu
    var = _dot_exact_rhs(yc * yc, hsum_ref[...], 1) * inv_n
    yc = yc * lax.rsqrt(var + GN_EPS) * lng_ref[...] + lnb_ref[...]
    yc = (yc + bonus_ref[...]) * g_ref[...]
    branches = ((ya_ref, pa_ref), (yb_ref, pb_ref), (yc, pc_ref))
    merged = None
    for bi, (y_b, p_ref) in enumerate(branches):
        y_b = y_b if bi == 2 else y_b[...]
        gate = jax.nn.sigmoid(_dot_nt(h, wg_ref[bi * d:(bi + 1) * d, :]) + gb_ref[bi:bi + 1, :])
        term = gate * _dot(y_b.astype(BF16), p_ref[...])
        merged = term if merged is None else merged + term
    o_ref[...] = x + _dot(merged.astype(BF16), wo_ref[...])


def merge_out(x, g_norm, w_g, ya, yb, yc, bonus, g, gate_bias, lnx_g, lnx_b, head_sum,
              p_a, p_b, p_c, w_out, tm):
    n, d = x.shape
    tile = lambda a: pl.BlockSpec((tm, a.shape[1]), lambda i: (i, 0))
    const = lambda a: _const_spec(a.shape)
    return pl.pallas_call(
        _merge_kernel,
        grid=(n // tm,),
        in_specs=[tile(x), const(g_norm), const(w_g), tile(ya), tile(yb), tile(yc), tile(bonus), tile(g),
                  const(gate_bias), const(lnx_g), const(lnx_b), const(head_sum),
                  const(p_a), const(p_b), const(p_c), const(w_out)],
        out_specs=pl.BlockSpec((tm, d), lambda i: (i, 0)),
        out_shape=jax.ShapeDtypeStruct((n, d), F32),
        compiler_params=_params("parallel"),
        name="merge_out",
    )(x, g_norm, w_g, ya, yb, yc, bonus, g, gate_bias, lnx_g, lnx_b, head_sum, p_a, p_b, p_c, w_out)


def _ffn_kernel(x_ref, g_ref, wg_ref, wu_ref, wd_ref, gf_ref, o_ref, h_scr, acc_scr, *, final_norm):
    j = pl.program_id(1)

    @pl.when(j == 0)
    def _():
        x = x_ref[...]
        h_scr[...] = _rmsnorm(x, g_ref[...]).astype(BF16)
        acc_scr[...] = x

    h = h_scr[...]
    act = jax.nn.silu(_dot(h, wg_ref[...])) * _dot(h, wu_ref[...])
    acc_scr[...] += _dot(act.astype(BF16), wd_ref[...])

    @pl.when(j == pl.num_programs(1) - 1)
    def _():
        y = acc_scr[...]
        o_ref[...] = _rmsnorm(y, gf_ref[...]) if final_norm else y


def ffn(x, g, w_gate_up, w_down, g_final, tm, tf, final_norm):
    n, d = x.shape
    dff = w_down.shape[0]
    nf = dff // tf
    return pl.pallas_call(
        functools.partial(_ffn_kernel, final_norm=final_norm),
        grid=(n // tm, nf),
        in_specs=[pl.BlockSpec((tm, d), lambda i, j: (i, 0)),
                  pl.BlockSpec((1, d), lambda i, j: (0, 0)),
                  pl.BlockSpec((d, tf), lambda i, j: (0, j)),
                  pl.BlockSpec((d, tf), lambda i, j: (0, j + nf)),
                  pl.BlockSpec((tf, d), lambda i, j: (j, 0)),
                  pl.BlockSpec((1, d), lambda i, j: (0, 0))],
        out_specs=pl.BlockSpec((tm, d), lambda i, j: (i, 0)),
        out_shape=jax.ShapeDtypeStruct((n, d), F32),
        scratch_shapes=[pltpu.VMEM((tm, d), BF16), pltpu.VMEM((tm, d), F32)],
        compiler_params=_params("parallel", "arbitrary"),
        name="ffn",
    )(x, g, w_gate_up, w_gate_up, w_down, g_final)


def _pad_rows(w, mult):
    pad = (-w.shape[0]) % mult
    return jnp.pad(w, ((0, pad), (0, 0))) if pad else w


def _largest_tile(total, cap):
    t = cap
    while total % t:
        t //= 2
    return t


def kernel(x, norm_mix, w_in, gate_bias, a_ln_g, a_ln_b, a_w_s, a_b_s, b_f_bias, c_mu, c_w0, c_w_up, c_a0, c_a_up, c_g_up, c_k_k, c_k_a, c_r_k, c_lnx_g, c_lnx_b, c_v0, c_v_down, c_v_up, p_a, p_b, p_c, w_out, norm_ffn, w_gate_up, w_down, norm_final):
    bn, s, d = x.shape
    depth = w_in.shape[0]
    n = bn * s
    a_width = a_ln_g.shape[1]
    b_width = B_HEADS * B_HEAD_DIM
    c_width = C_HEADS * C_HEAD_DIM
    a_cols = 2 * a_width
    b_cols = 3 * b_width + B_HEADS
    c_cols = 3 * c_width + C_DECAY_LORA + C_AAA_LORA + C_GATE_LORA
    dff = w_down.shape[1]
    assert s % 128 == 0 and dff % 256 == 0

    tm = _largest_tile(n, 512)
    t_attn = _largest_tile(s, 512)
    t_chunk = _largest_tile(s, 256)
    t_state = _largest_tile(s, 512)
    tf = dff // 2 if (dff // 2) % 128 == 0 else dff

    head_sum = jnp.kron(jnp.eye(C_HEADS, dtype=BF16), jnp.ones((C_HEAD_DIM, C_HEAD_DIM), BF16))
    row2 = lambda p: p.reshape(1, -1)

    xf = x.reshape(n, d)
    w_in_t = jnp.swapaxes(w_in, 1, 2)
    v_first = None
    for l in range(depth):
        w_a = w_in_t[l, :a_cols].astype(BF16)
        w_b = _pad_rows(w_in_t[l, a_cols:a_cols + b_cols].astype(BF16), 128)
        w_c = w_in_t[l, a_cols + b_cols:a_cols + b_cols + c_cols].astype(BF16)
        w_g = w_in_t[l, a_cols + b_cols + c_cols:].astype(BF16)
        g_mix = row2(norm_mix[l])

        b_s_full = jnp.repeat(a_b_s[l].T, a_width // A_GROUPS, axis=1)
        ya = gmlp(xf, g_mix, w_a, row2(a_ln_g[l]), row2(a_ln_b[l]), a_w_s[l], b_s_full, tm)

        f_bias_row = jnp.pad(b_f_bias[l], (0, 128 - B_HEADS)).reshape(1, 128)
        qa, ka, vt = fox_prep(xf, g_mix, w_b, f_bias_row, s, tm)
        yb = fox_attention(qa, ka, vt, bn, s, tm, FOX_HEAD_GROUP)

        zeros_lora = jnp.zeros((C_AAA_LORA, c_width), F32)
        w_up_pad = jnp.concatenate([c_w_up[l], zeros_lora], axis=0)
        a_up_pad = jnp.concatenate([jnp.zeros((C_DECAY_LORA, c_width), F32), c_a_up[l]], axis=0)
        mix = {} if l == 0 else dict(v_first=v_first, v0=row2(c_v0[l - 1]),
                                     v_down=c_v_down[l - 1], v_up=c_v_up[l - 1])
        r_c, ld_c, k_c, v_c, kk_c, b_c, g_c, bonus_c = rwkv_prep(
            xf, g_mix, w_c, row2(c_mu[l]), row2(c_w0[l]), w_up_pad, row2(c_a0[l]), a_up_pad, c_g_up[l],
            row2(c_k_k[l]), row2(c_k_a[l]), row2(c_r_k[l]), head_sum, s, tm, **mix)
        if l == 0:
            v_first = v_c
        rq, y0, g_mat, h_mat = rwkv_chunk(r_c, ld_c, k_c, v_c, kk_c, b_c, t_chunk)
        per_batch = lambda t: t.reshape(bn, s, c_width)
        yc = rwkv_state(per_batch(rq), per_batch(y0), per_batch(g_mat), per_batch(h_mat), t_state)
        yc = yc.reshape(n, c_width)

        xf = merge_out(xf, g_mix, w_g, ya, yb, yc, bonus_c, g_c, gate_bias[l], row2(c_lnx_g[l]),
                       row2(c_lnx_b[l]), head_sum, p_a[l].astype(BF16), p_b[l].astype(BF16),
                       p_c[l].astype(BF16), w_out[l].astype(BF16), tm)
        xf = ffn(xf, row2(norm_ffn[l]), w_gate_up[l].astype(BF16), w_down[l].astype(BF16),
                 row2(norm_final), _largest_tile(n, 1024), tf, final_norm=(l == depth - 1))
    return xf.reshape(bn, s, d)
```

```python
import functools

import jax
import jax.numpy as jnp
from jax import lax
from jax.experimental import pallas as pl
from jax.experimental.pallas import tpu as pltpu

F32 = jnp.float32
BF16 = jnp.bfloat16

V7X_VMEM_LIMIT_BYTES = 56 * 1024 * 1024

NORM_EPS = 1e-6
LN_EPS = 1e-5
GN_EPS = 64e-5

N_BRANCH = 3
A_GROUPS = 4
A_CHUNK = 128
B_HEADS = 8
B_HEAD_DIM = 64
C_HEADS = 8
C_HEAD_DIM = 64
C_DECAY_LORA = 64
C_AAA_LORA = 64
C_GATE_LORA = 128
PROJECT_ROWS = 256
LOG2_E = 1.4426950408889634
DECAY_SCALE = 0.6065306597126334
FOX_HEAD_GROUP = 4
FOX_V_ROWS = 80
RWKV_CHUNK = 64
RWKV_PACK = 2


def _params(*semantics):
    return pltpu.CompilerParams(dimension_semantics=semantics,
                                vmem_limit_bytes=V7X_VMEM_LIMIT_BYTES)


def _rmsnorm(x, g):
    return x * lax.rsqrt(jnp.mean(x * x, axis=-1, keepdims=True) + NORM_EPS) * g


def _dot(a, b, precision=None):
    return jnp.dot(a, b, preferred_element_type=F32, precision=precision)


def _dot_nt(a, b, precision=None):
    return lax.dot_general(a, b, (((1,), (1,)), ((), ())),
                           preferred_element_type=F32, precision=precision)


def _bf16_terms(x, n):
    terms = []
    for _ in range(n - 1):
        t = x.astype(BF16)
        terms.append(t)
        x = x - t.astype(F32)
    terms.append(x.astype(BF16))
    return terms


def _dot_exact_rhs(x, m, n_terms):
    return sum(_dot(t, m) for t in _bf16_terms(x, n_terms))


def _dot_exact_lhs(m, x, n_terms):
    return sum(_dot(m, t) for t in _bf16_terms(x, n_terms))


def _dot_3pass(a, b):
    a_hi, a_lo = _bf16_terms(a, 2)
    b_hi, b_lo = _bf16_terms(b, 2)
    return _dot(a_hi, b_hi) + (_dot(a_lo, b_hi) + _dot(a_hi, b_lo))


def _dot_tn(a, b, precision=None):
    return lax.dot_general(a, b, (((0,), (0,)), ((), ())),
                           preferred_element_type=F32, precision=precision)


def _const_spec(shape):
    return pl.BlockSpec(shape, lambda *_: (0,) * len(shape), pipeline_mode=pl.Buffered(1))


def _project(x_ref, g_ref, wt_ref, rows=slice(None)):
    return _dot_nt(_rmsnorm(x_ref[rows, :], g_ref[...]).astype(BF16), wt_ref[...])


def _interleaved(n_parts, produce, consume):
    ready = produce(0)
    for k in range(n_parts):
        upcoming = produce(k + 1) if k + 1 < n_parts else None
        consume(k, ready)
        ready = upcoming


def _gmlp_kernel(x_ref, gn_ref, w_ref, lng_ref, lnb_ref, ws_ref, bs_ref, o_ref, *, width):
    tm = x_ref.shape[0]
    gd = width // A_GROUPS
    part = min(tm, PROJECT_ROWS)
    row = lax.broadcasted_iota(jnp.int32, (A_CHUNK, A_CHUNK), 0)
    col = lax.broadcasted_iota(jnp.int32, (A_CHUNK, A_CHUNK), 1)
    causal = col <= row

    def mix(k, pa):
        for c in range(part // A_CHUNK):
            rows = slice(c * A_CHUNK, (c + 1) * A_CHUNK)
            out_rows = slice(k * part + c * A_CHUNK, k * part + (c + 1) * A_CHUNK)
            act = jax.nn.gelu(pa[rows, :])
            u = act[:, :width]
            v = act[:, width:]
            mu = jnp.mean(v, axis=-1, keepdims=True)
            var = jnp.mean(jnp.square(v - mu), axis=-1, keepdims=True)
            v = (v - mu) * lax.rsqrt(var + LN_EPS) * lng_ref[...] + lnb_ref[...]
            v = v.astype(BF16)
            for g in range(A_GROUPS):
                cols = slice(g * gd, (g + 1) * gd)
                w = jnp.where(causal, ws_ref[g], 0.0).astype(BF16)
                mixed = _dot(w, v[:, cols]) + bs_ref[:, cols]
                o_ref[out_rows, cols] = (u[:, cols] * mixed).astype(o_ref.dtype)

    _interleaved(tm // part,
                 lambda k: _project(x_ref, gn_ref, w_ref, slice(k * part, (k + 1) * part)), mix)


def gmlp(x, g_norm, w_a, ln_g, ln_b, w_s, b_s_full, tm):
    n, d = x.shape
    width = w_a.shape[0] // 2
    return pl.pallas_call(
        functools.partial(_gmlp_kernel, width=width),
        grid=(n // tm,),
        in_specs=[pl.BlockSpec((tm, d), lambda i: (i, 0)),
                  _const_spec((1, d)), _const_spec(w_a.shape),
                  _const_spec((1, width)), _const_spec((1, width)),
                  _const_spec((A_GROUPS, A_CHUNK, A_CHUNK)), _const_spec((A_CHUNK, width))],
        out_specs=pl.BlockSpec((tm, width), lambda i: (i, 0)),
        out_shape=jax.ShapeDtypeStruct((n, width), BF16),
        compiler_params=_params("parallel"),
        name="gmlp",
    )(x, g_norm, w_a, ln_g, ln_b, w_s, b_s_full)


def _one_hot(shape, hit):
    r = lax.broadcasted_iota(jnp.int32, shape, 0)
    c = lax.broadcasted_iota(jnp.int32, shape, 1)
    return jnp.where(hit(r, c), 1.0, 0.0).astype(BF16)


def _fox_prep_kernel(x_ref, gn_ref, w_ref, fb_ref, qa_ref, ka_ref, vt_ref, carry_scr, *, seq, scale):
    tm = x_ref.shape[0]
    hd = B_HEAD_DIM
    width = B_HEADS * hd
    part = min(tm, PROJECT_ROWS)

    @pl.when((pl.program_id(0) * tm) % seq == 0)
    def _():
        carry_scr[...] = jnp.zeros(carry_scr.shape, F32)

    tril = (lax.broadcasted_iota(jnp.int32, (part, part), 1)
            <= lax.broadcasted_iota(jnp.int32, (part, part), 0)).astype(BF16)
    lane = lax.broadcasted_iota(jnp.int32, (part, 128), 1)
    low = lane < hd
    ones_lanes = (lane >= hd) & (lane < hd + 3)
    v_rows_n = vt_ref.shape[2]
    row_t = lax.broadcasted_iota(jnp.int32, (v_rows_n, part), 0)
    place_c = _one_hot((128, B_HEADS * 128),
                       lambda r, c: (r < 3 * B_HEADS) & (c == 128 * (r % B_HEADS) + hd + r // B_HEADS))

    def derive(part_idx, x):
        rows = slice(part_idx * part, (part_idx + 1) * part)
        log_f = jax.nn.log_sigmoid(x[:, 3 * width:3 * width + 128] + fb_ref[...]) * LOG2_E
        cum = _dot_exact_lhs(tril, log_f, 3) + carry_scr[...]
        carry_scr[...] = cum[part - 1:part, :]
        terms = _bf16_terms(jnp.where(lane < B_HEADS, -cum, 0.0), 3)
        c_terms = sum(pltpu.roll(t.astype(F32), B_HEADS * i, axis=1) if i else t.astype(F32)
                      for i, t in enumerate(terms)).astype(BF16)
        c_placed = _dot(c_terms, place_c)
        for pair in range(B_HEADS // 2):
            lanes = slice(pair * 128, pair * 128 + 128)
            q = x[:, lanes] * (scale * LOG2_E)
            k = x[:, width:2 * width][:, lanes]
            v = x[:, 2 * width:3 * width][:, lanes].astype(BF16)
            for odd in range(2):
                h = 2 * pair + odd
                q_h, k_h = (pltpu.roll(q, hd, axis=1), pltpu.roll(k, hd, axis=1)) if odd else (q, k)
                v_rows = _one_hot((v_rows_n, 128), lambda r, c: (r < hd) & (c == r + odd * hd))
                qa_ref[h, rows, :] = jnp.where(ones_lanes, 1.0, jnp.where(low, q_h, 0.0)).astype(BF16)
                ka_ref[h, rows, :] = (jnp.where(low, k_h, 0.0)
                                      + c_placed[:, h * 128:(h + 1) * 128]).astype(BF16)
                vt_ref[h, 0, :, rows] = jnp.where(row_t == hd, 1.0, _dot_nt(v_rows, v)).astype(BF16)

    _interleaved(tm // part,
                 lambda k: _project(x_ref, gn_ref, w_ref, slice(k * part, (k + 1) * part)), derive)


def fox_prep(x, g_norm, w_b, f_bias_row, seq, tm):
    n, d = x.shape
    return pl.pallas_call(
        functools.partial(_fox_prep_kernel, seq=seq, scale=B_HEAD_DIM ** -0.5),
        grid=(n // tm,),
        in_specs=[pl.BlockSpec((tm, d), lambda i: (i, 0)),
                  _const_spec((1, d)), _const_spec(w_b.shape), _const_spec((1, 128))],
        out_specs=[pl.BlockSpec((B_HEADS, tm, 128), lambda i: (0, i, 0)),
                   pl.BlockSpec((B_HEADS, tm, 128), lambda i: (0, i, 0)),
                   pl.BlockSpec((B_HEADS, 1, FOX_V_ROWS, tm), lambda i: (0, i, 0, 0))],
        out_shape=[jax.ShapeDtypeStruct((B_HEADS, n, 128), BF16),
                   jax.ShapeDtypeStruct((B_HEADS, n, 128), BF16),
                   jax.ShapeDtypeStruct((B_HEADS, n // tm, FOX_V_ROWS, tm), BF16)],
        scratch_shapes=[pltpu.VMEM((1, 128), F32)],
        compiler_params=_params("arbitrary"),
        name="fox_prep",
    )(x, g_norm, w_b, f_bias_row)


def _fox_kernel(qa_ref, ka_ref, vt_ref, o_ref, m_scr, acc_scr, s_scr, smax_scr):
    i = pl.program_id(2)
    heads, tq, _ = qa_ref.shape
    tk = vt_ref.shape[3]
    hd = B_HEAD_DIM
    m_scr[...] = jnp.full(m_scr.shape, -jnp.inf, F32)
    acc_scr[...] = jnp.zeros(acc_scr.shape, F32)

    hs = range(heads)

    def logits(j, slot, h):
        start = pl.multiple_of(j * tk, tk)
        s = _dot_nt(ka_ref[h, pl.ds(start, tk), :], qa_ref[h])
        s_scr[slot, h] = s
        smax_scr[slot, h] = jnp.max(s, axis=0, keepdims=True)

    def consume(slot, j, h, masked):
        s = s_scr[slot, h]
        if masked:
            key = lax.broadcasted_iota(jnp.int32, (tk, tq), 0)
            qry = lax.broadcasted_iota(jnp.int32, (tk, tq), 1)
            s = jnp.where(key <= qry, s, -jnp.inf)
            s_max = jnp.max(s, axis=0, keepdims=True)
        else:
            s_max = smax_scr[slot, h]
        m_prev = m_scr[h]
        m_new = jnp.maximum(m_prev, s_max)
        p = jnp.exp2((s - m_new).astype(BF16))
        acc_scr[h] = jnp.exp2(m_prev - m_new) * acc_scr[h] + _dot(vt_ref[h, j], p)
        m_scr[h] = m_new

    def advance(cur, cur_slot, masked=False, nxt=None):
        for h in hs:
            if nxt is not None:
                logits(nxt, 1 - cur_slot, h)
            consume(cur_slot, cur, h, masked)

    for h in hs:
        logits(0, 0, h)

    def body(jj, carry):
        a = 2 * jj
        advance(a, 0, nxt=a + 1)
        advance(a + 1, 1, nxt=a + 2)
        return carry

    lax.fori_loop(0, i // 2, body, 0)

    @pl.when(i % 2 == 0)
    def _():
        advance(i, 0, masked=True)

    @pl.when(i % 2 == 1)
    def _():
        advance(i - 1, 0, nxt=i)
        advance(i, 1, masked=True)

    out_t = [acc_scr[h, 0:hd, :] / acc_scr[h, hd:hd + 1, :] for h in range(heads)]
    o_ref[...] = jnp.concatenate(out_t, axis=0).T.astype(o_ref.dtype)


def fox_attention(qa, ka, vt, bn, seq, tq, group):
    heads, n, _ = qa.shape
    rows = vt.shape[2]
    nq = seq // tq
    return pl.pallas_call(
        _fox_kernel,
        grid=(bn, heads // group, nq),
        in_specs=[pl.BlockSpec((group, tq, 128), lambda bi, hi, i: (hi, bi * nq + i, 0)),
                  pl.BlockSpec((group, seq, 128), lambda bi, hi, i: (hi, bi, 0)),
                  pl.BlockSpec((group, nq, rows, tq), lambda bi, hi, i: (hi, bi, 0, 0))],
        out_specs=pl.BlockSpec((tq, group * B_HEAD_DIM), lambda bi, hi, i: (bi * nq + i, hi)),
        out_shape=jax.ShapeDtypeStruct((n, heads * B_HEAD_DIM), BF16),
        scratch_shapes=[pltpu.VMEM((group, 1, tq), F32), pltpu.VMEM((group, rows, tq), F32),
                        pltpu.VMEM((2, group, tq, tq), F32), pltpu.VMEM((2, group, 1, tq), F32)],
        compiler_params=_params("parallel", "parallel", "arbitrary"),
        name="fox_attention",
    )(qa, ka, vt)


def _rwkv_prep_kernel(*refs, seq, mix_value):
    (x_ref, gn_ref, w_ref, mu_ref, w0_ref, wup_ref, a0_ref, aup_ref, gup_ref, kk_ref, ka_ref,
     rk_ref, hsum_ref) = refs[:13]
    if mix_value:
        vfirst_ref, v0_ref, vdown_ref, vup_ref = refs[13:17]
    r_out, ld_out, k_out, v_out, kk_out, b_out, g_out, bonus_out, last_scr = refs[-9:]
    tm = x_ref.shape[0]
    cw = r_out.shape[1]
    part = min(tm, PROJECT_ROWS)

    @pl.when((pl.program_id(0) * tm) % seq == 0)
    def _():
        last_scr[...] = jnp.zeros(last_scr.shape, F32)

    def derive(part_idx, pc):
        rows = slice(part_idx * part, (part_idx + 1) * part)
        prev_row = last_scr[...]
        last_scr[...] = pc[part - 1:part, :]
        row = lax.broadcasted_iota(jnp.int32, pc.shape, 0)
        shifted = jnp.where(row == 0, prev_row, pltpu.roll(pc, 1, axis=0))
        pc = pc + (shifted - pc) * mu_ref[...]

        r = pc[:, 0:cw]
        k = pc[:, cw:2 * cw]
        v = pc[:, 2 * cw:3 * cw]
        lora = pc[:, 3 * cw:3 * cw + C_DECAY_LORA + C_AAA_LORA]
        g_lo = pc[:, 3 * cw + C_DECAY_LORA + C_AAA_LORA:]
        if mix_value:
            lam = jax.nn.sigmoid(v0_ref[...] + _mm(_mm(v, vdown_ref[...]), vup_ref[...]))
            v = v + (vfirst_ref[rows, :] - v) * lam
        z = w0_ref[...] + _dot_3pass(jnp.tanh(lora), wup_ref[...])
        log_decay = -DECAY_SCALE * jax.nn.sigmoid(z)
        a = jax.nn.sigmoid(a0_ref[...] + _mm(lora, aup_ref[...]))
        g = _mm(jax.nn.sigmoid(g_lo), gup_ref[...])
        kk = k * kk_ref[...]
        ss = _dot_exact_rhs(kk * kk, hsum_ref[...], 1)
        kk = kk / jnp.maximum(jnp.sqrt(ss), 1e-12)
        k = k * (1.0 + (a - 1.0) * ka_ref[...])
        r_out[rows, :] = r
        ld_out[rows, :] = log_decay
        k_out[rows, :] = k
        v_out[rows, :] = v
        kk_out[rows, :] = kk
        b_out[rows, :] = kk * a
        g_out[rows, :] = g
        bonus_out[rows, :] = _dot_exact_rhs(r * k * rk_ref[...], hsum_ref[...], 1) * v

    _interleaved(tm // part,
                 lambda k: _project(x_ref, gn_ref, w_ref, slice(k * part, (k + 1) * part)), derive)


def rwkv_prep(x, g_norm, w_c, mu, w0, w_up_pad, a0, a_up_pad, g_up, k_k, k_a, r_k, head_sum, seq, tm,
              v_first=None, v0=None, v_down=None, v_up=None):
    n, d = x.shape
    ccols = w_c.shape[0]
    cw = w0.shape[1]
    mix_value = v_first is not None
    const = _const_spec
    in_specs = [pl.BlockSpec((tm, d), lambda i: (i, 0)), const((1, d)), const(w_c.shape),
                const((1, ccols)), const((1, cw)), const(w_up_pad.shape), const((1, cw)),
                const(a_up_pad.shape), const(g_up.shape), const((1, cw)), const((1, cw)),
                const((1, cw)), const(head_sum.shape)]
    args = [x, g_norm, w_c, mu, w0, w_up_pad, a0, a_up_pad, g_up, k_k, k_a, r_k, head_sum]
    if mix_value:
        in_specs += [pl.BlockSpec((tm, cw), lambda i: (i, 0)), const((1, cw)),
                     const(v_down.shape), const(v_up.shape)]
        args += [v_first, v0, v_down, v_up]
    out = jax.ShapeDtypeStruct((n, cw), F32)
    return pl.pallas_call(
        functools.partial(_rwkv_prep_kernel, seq=seq, mix_value=mix_value),
        grid=(n // tm,),
        in_specs=in_specs,
        out_specs=[pl.BlockSpec((tm, cw), lambda i: (i, 0))] * 8,
        out_shape=[out] * 8,
        scratch_shapes=[pltpu.VMEM((1, ccols), F32)],
        compiler_params=_params("arbitrary"),
        name="rwkv_prep",
    )(*args)


def _lane_head(shape):
    return lax.broadcasted_iota(jnp.int32, shape, 1) // C_HEAD_DIM


def _stack_masked(x, pack):
    head = _lane_head(x.shape)
    return jnp.concatenate([jnp.where(head == h, x, 0.0) for h in range(pack)], axis=0)


def _diag_blocks(full, pack):
    n = C_HEAD_DIM
    head = _lane_head((n, pack * n))
    out = full[0:n]
    for h in range(1, pack):
        out = jnp.where(head == h, full[h * n:(h + 1) * n], out)
    return out


def _mm(a, b):
    return _dot(a.astype(BF16), b.astype(BF16))


def _mm_nt(a, b):
    return _dot_nt(a.astype(BF16), b.astype(BF16))


def _mm_tn(a, b):
    return _dot_tn(a.astype(BF16), b.astype(BF16))


def _rwkv_chunk_kernel(r_ref, ld_ref, k_ref, v_ref, kk_ref, b_ref,
                       rq_out, y0_out, g_out, h_out, *, pack):
    tc, cw = r_ref.shape
    c = RWKV_CHUNK
    pw = pack * C_HEAD_DIM
    row = lax.broadcasted_iota(jnp.int32, (c, pw), 0)
    col = lax.broadcasted_iota(jnp.int32, (c, pw), 1) % C_HEAD_DIM
    incl = col <= row
    strict = col < row
    eye = (col == row).astype(F32)
    tril_ones = (lax.broadcasted_iota(jnp.int32, (c, c), 1)
                 <= lax.broadcasted_iota(jnp.int32, (c, c), 0)).astype(BF16)

    units = [(slice(ci * c, (ci + 1) * c), slice(p * pw, (p + 1) * pw))
             for ci in range(tc // c) for p in range(cw // pw)]
    each = lambda fn, *lists: [fn(*args) for args in zip(*lists)]
    stack = lambda x: _stack_masked(x, pack)
    diag = lambda x: _diag_blocks(x, pack)

    cums = {}
    for rows, _ in units:
        if rows.start not in cums:
            cums[rows.start] = _dot_exact_lhs(tril_ones, ld_ref[rows, :], 3)
    cum = [cums[rows.start][:, lanes] for rows, lanes in units]
    ld = [ld_ref[u] for u in units]
    v = [v_ref[u] for u in units]
    p_incl = each(jnp.exp, cum)
    p_inv = each(lambda x: jnp.exp(-x), cum)
    p_prev = each(lambda x, y: jnp.exp(x - y), cum, ld)
    r_t = each(lambda u, s: r_ref[u] * s, units, p_incl)
    kk_t = each(lambda u, s: kk_ref[u] * s, units, p_prev)
    b_t = each(lambda u, s: b_ref[u] * s, units, p_inv)
    k_t = each(lambda u, s: k_ref[u] * s, units, p_inv)

    lhs = each(lambda x, y: jnp.concatenate([x, y], axis=0).astype(BF16), kk_t, r_t)
    rhs_bk = each(lambda x, y: jnp.concatenate([stack(x), stack(y)], axis=0).astype(BF16), b_t, k_t)
    with_bk = each(_dot_nt, lhs, rhs_bk)
    a_b = each(lambda x: jnp.where(strict, x[:c, :pw], 0.0), with_bk)
    a_rb = each(lambda x: jnp.where(incl, x[c:, :pw], 0.0).astype(BF16), with_bk)
    a_kk = each(lambda x: jnp.concatenate([jnp.where(strict, x[:c, pw:], 0.0),
                                           jnp.where(incl, x[c:, pw:], 0.0)], axis=0), with_bk)
    av = each(lambda x, y: _mm(x, stack(y)), a_kk, v)

    lower_left = lambda m: (((row // (2 * m)) == (col // (2 * m)))
                            & ((row // m) % 2 == 1) & ((col // m) % 2 == 0))
    t_inv = each(lambda x: eye - jnp.where(lower_left(1), x, 0.0), a_b)
    m = 2
    while m < c:
        mask = lower_left(m)
        xl = each(lambda x, y: _mm(x, stack(jnp.where(mask, y, 0.0))), t_inv, a_b)
        t_inv = each(lambda x, y: x - _mm(y, stack(x)), t_inv, xl)
        m *= 2

    t_bf = each(lambda x: x.astype(BF16), t_inv)
    side_by_side = lambda x, y: jnp.concatenate([stack(x), stack(y)], axis=1).astype(BF16)
    wu = each(lambda t, x, y: _dot(t, side_by_side(x, y[:c])), t_bf, kk_t, av)
    w = each(lambda x: x[:, :pw], wu)
    u0 = each(lambda x: x[:, pw:], wu)
    for i, u in enumerate(units):
        corr = _dot(a_rb[i], side_by_side(w[i], u0[i]))
        rq_out[u] = (r_t[i] - corr[:, :pw]).astype(rq_out.dtype)
        y0_out[u] = av[i][c:] - corr[:, pw:]
    for i, u in enumerate(units):
        p_end = p_incl[i][c - 1:c, :]
        wu_b = _mm_tn(wu[i], b_t[i])
        g_out[u] = ((eye - diag(wu_b[:pw])) * p_end).astype(g_out.dtype)
        h_out[u] = (diag(_mm_tn(v[i], k_t[i])) - diag(wu_b[pw:])) * p_end


def rwkv_chunk(r, ld, k, v, kk, b, tc):
    n, cw = r.shape
    spec = pl.BlockSpec((tc, cw), lambda i: (i, 0))
    out = jax.ShapeDtypeStruct((n, cw), F32)
    return pl.pallas_call(
        functools.partial(_rwkv_chunk_kernel, pack=RWKV_PACK),
        grid=(n // tc,),
        in_specs=[spec] * 6,
        out_specs=[spec] * 4,
        out_shape=[jax.ShapeDtypeStruct((n, cw), BF16), out, jax.ShapeDtypeStruct((n, cw), BF16), out],
        compiler_params=_params("parallel"),
        name="rwkv_chunk",
    )(r, ld, k, v, kk, b)


def _rwkv_state_kernel(rq_ref, y0_ref, g_ref, h_ref, y_out, state_scr, *, pack):
    bn, tc, cw = rq_ref.shape
    c = RWKV_CHUNK
    pw = pack * C_HEAD_DIM
    groups = cw // pw

    @pl.when(pl.program_id(0) == 0)
    def _():
        state_scr[...] = jnp.zeros(state_scr.shape, F32)

    chains = [(bi, p, slice(p * pw, (p + 1) * pw)) for bi in range(bn) for p in range(groups)]
    state = [state_scr[bi, p] for bi, p, _ in chains]
    for ci in range(tc // c):
        rows = slice(ci * c, (ci + 1) * c)
        for i, (bi, p, lanes) in enumerate(chains):
            y_out[bi, rows, lanes] = (_mm_nt(rq_ref[bi, rows, lanes], _stack_masked(state[i], pack))
                                      + y0_ref[bi, rows, lanes])
        for i, (bi, p, lanes) in enumerate(chains):
            state[i] = _mm(state[i], _stack_masked(g_ref[bi, rows, lanes], pack)) + h_ref[bi, rows, lanes]
    for i, (bi, p, _) in enumerate(chains):
        state_scr[bi, p] = state[i]


def rwkv_state(rq, y0, g, h, tc):
    bn, s, cw = rq.shape
    pw = RWKV_PACK * C_HEAD_DIM
    spec = pl.BlockSpec((bn, tc, cw), lambda t: (0, t, 0))
    return pl.pallas_call(
        functools.partial(_rwkv_state_kernel, pack=RWKV_PACK),
        grid=(s // tc,),
        in_specs=[spec] * 4,
        out_specs=spec,
        out_shape=jax.ShapeDtypeStruct(rq.shape, F32),
        scratch_shapes=[pltpu.VMEM((bn, cw // pw, C_HEAD_DIM, pw), F32)],
        compiler_params=_params("arbitrary"),
        name="rwkv_state",
    )(rq, y0, g, h)


def _merge_kernel(x_ref, gn_ref, wg_ref, ya_ref, yb_ref, yc_ref, bonus_ref, g_ref, gb_ref, lng_ref, lnb_ref,
                  hsum_ref, pa_ref, pb_ref, pc_ref, wo_ref, o_ref):
    d = x_ref.shape[1]
    x = x_ref[...]
    h = _rmsnorm(x, gn_ref[...]).astype(BF16)
    y = yc_ref[...]
    inv_n = 1.0 / C_HEAD_DIM
    mu = _dot_exact_rhs(y, hsum_ref[...], 1) * inv_n
    yc = y - mu
    var = _dot_exact_rhs(yc * yc, hsum_ref[...], 1) * inv_n
    yc = yc * lax.rsqrt(var + GN_EPS) * lng_ref[...] + lnb_ref[...]
    yc = (yc + bonus_ref[...]) * g_ref[...]
    branches = ((ya_ref, pa_ref), (yb_ref, pb_ref), (yc, pc_ref))
    merged = None
    for bi, (y_b, p_ref) in enumerate(branches):
        y_b = y_b if bi == 2 else y_b[...]
        gate = jax.nn.sigmoid(_dot_nt(h, wg_ref[bi * d:(bi + 1) * d, :]) + gb_ref[bi:bi + 1, :])
        term = gate * _dot(y_b.astype(BF16), p_ref[...])
        merged = term if merged is None else merged + term
    o_ref[...] = x + _dot(merged.astype(BF16), wo_ref[...])


def merge_out(x, g_norm, w_g, ya, yb, yc, bonus, g, gate_bias, lnx_g, lnx_b, head_sum,
              p_a, p_b, p_c, w_out, tm):
    n, d = x.shape
    tile = lambda a: pl.BlockSpec((tm, a.shape[1]), lambda i: (i, 0))
    const = lambda a: _const_spec(a.shape)
    return pl.pallas_call(
        _merge_kernel,
        grid=(n // tm,),
        in_specs=[tile(x), const(g_norm), const(w_g), tile(ya), tile(yb), tile(yc), tile(bonus), tile(g),
                  const(gate_bias), const(lnx_g), const(lnx_b), const(head_sum),
                  const(p_a), const(p_b), const(p_c), const(w_out)],
        out_specs=pl.BlockSpec((tm, d), lambda i: (i, 0)),
        out_shape=jax.ShapeDtypeStruct((n, d), F32),
        compiler_params=_params("parallel"),
        name="merge_out",
    )(x, g_norm, w_g, ya, yb, yc, bonus, g, gate_bias, lnx_g, lnx_b, head_sum, p_a, p_b, p_c, w_out)


def _ffn_kernel(x_ref, g_ref, wg_ref, wu_ref, wd_ref, gf_ref, o_ref, h_scr, acc_scr, *, final_norm):
    j = pl.program_id(1)

    @pl.when(j == 0)
    def _():
        x = x_ref[...]
        h_scr[...] = _rmsnorm(x, g_ref[...]).astype(BF16)
        acc_scr[...] = x

    h = h_scr[...]
    act = jax.nn.silu(_dot(h, wg_ref[...])) * _dot(h, wu_ref[...])
    acc_scr[...] += _dot(act.astype(BF16), wd_ref[...])

    @pl.when(j == pl.num_programs(1) - 1)
    def _():
        y = acc_scr[...]
        o_ref[...] = _rmsnorm(y, gf_ref[...]) if final_norm else y


def ffn(x, g, w_gate_up, w_down, g_final, tm, tf, final_norm):
    n, d = x.shape
    dff = w_down.shape[0]
    nf = dff // tf
    return pl.pallas_call(
        functools.partial(_ffn_kernel, final_norm=final_norm),
        grid=(n // tm, nf),
        in_specs=[pl.BlockSpec((tm, d), lambda i, j: (i, 0)),
                  pl.BlockSpec((1, d), lambda i, j: (0, 0)),
                  pl.BlockSpec((d, tf), lambda i, j: (0, j)),
                  pl.BlockSpec((d, tf), lambda i, j: (0, j + nf)),
                  pl.BlockSpec((tf, d), lambda i, j: (j, 0)),
                  pl.BlockSpec((1, d), lambda i, j: (0, 0))],
        out_specs=pl.BlockSpec((tm, d), lambda i, j: (i, 0)),
        out_shape=jax.ShapeDtypeStruct((n, d), F32),
        scratch_shapes=[pltpu.VMEM((tm, d), BF16), pltpu.VMEM((tm, d), F32)],
        compiler_params=_params("parallel", "arbitrary"),
        name="ffn",
    )(x, g, w_gate_up, w_gate_up, w_down, g_final)


def _pad_rows(w, mult):
    pad = (-w.shape[0]) % mult
    return jnp.pad(w, ((0, pad), (0, 0))) if pad else w


def _largest_tile(total, cap):
    t = cap
    while total % t:
        t //= 2
    return t


def kernel(x, norm_mix, w_in, gate_bias, a_ln_g, a_ln_b, a_w_s, a_b_s, b_f_bias, c_mu, c_w0, c_w_up, c_a0, c_a_up, c_g_up, c_k_k, c_k_a, c_r_k, c_lnx_g, c_lnx_b, c_v0, c_v_down, c_v_up, p_a, p_b, p_c, w_out, norm_ffn, w_gate_up, w_down, norm_final):
    bn, s, d = x.shape
    depth = w_in.shape[0]
    n = bn * s
    a_width = a_ln_g.shape[1]
    b_width = B_HEADS * B_HEAD_DIM
    c_width = C_HEADS * C_HEAD_DIM
    a_cols = 2 * a_width
    b_cols = 3 * b_width + B_HEADS
    c_cols = 3 * c_width + C_DECAY_LORA + C_AAA_LORA + C_GATE_LORA
    dff = w_down.shape[1]
    assert s % 128 == 0 and dff % 256 == 0

    tm = _largest_tile(n, 512)
    t_attn = _largest_tile(s, 512)
    t_chunk = _largest_tile(s, 512)
    t_state = _largest_tile(s, 512)
    tf = dff // 2 if (dff // 2) % 128 == 0 else dff

    head_sum = jnp.kron(jnp.eye(C_HEADS, dtype=BF16), jnp.ones((C_HEAD_DIM, C_HEAD_DIM), BF16))
    row2 = lambda p: p.reshape(1, -1)

    xf = x.reshape(n, d)
    w_in_t = jnp.swapaxes(w_in, 1, 2)
    v_first = None
    for l in range(depth):
        w_a = w_in_t[l, :a_cols].astype(BF16)
        w_b = _pad_rows(w_in_t[l, a_cols:a_cols + b_cols].astype(BF16), 128)
        w_c = w_in_t[l, a_cols + b_cols:a_cols + b_cols + c_cols].astype(BF16)
        w_g = w_in_t[l, a_cols + b_cols + c_cols:].astype(BF16)
        g_mix = row2(norm_mix[l])

        b_s_full = jnp.repeat(a_b_s[l].T, a_width // A_GROUPS, axis=1)
        ya = gmlp(xf, g_mix, w_a, row2(a_ln_g[l]), row2(a_ln_b[l]), a_w_s[l], b_s_full, tm)

        f_bias_row = jnp.pad(b_f_bias[l], (0, 128 - B_HEADS)).reshape(1, 128)
        qa, ka, vt = fox_prep(xf, g_mix, w_b, f_bias_row, s, tm)
        yb = fox_attention(qa, ka, vt, bn, s, tm, FOX_HEAD_GROUP)

        zeros_lora = jnp.zeros((C_AAA_LORA, c_width), F32)
        w_up_pad = jnp.concatenate([c_w_up[l], zeros_lora], axis=0)
        a_up_pad = jnp.concatenate([jnp.zeros((C_DECAY_LORA, c_width), F32), c_a_up[l]], axis=0)
        mix = {} if l == 0 else dict(v_first=v_first, v0=row2(c_v0[l - 1]),
                                     v_down=c_v_down[l - 1], v_up=c_v_up[l - 1])
        r_c, ld_c, k_c, v_c, kk_c, b_c, g_c, bonus_c = rwkv_prep(
            xf, g_mix, w_c, row2(c_mu[l]), row2(c_w0[l]), w_up_pad, row2(c_a0[l]), a_up_pad, c_g_up[l],
            row2(c_k_k[l]), row2(c_k_a[l]), row2(c_r_k[l]), head_sum, s, tm, **mix)
        if l == 0:
            v_first = v_c
        rq, y0, g_mat, h_mat = rwkv_chunk(r_c, ld_c, k_c, v_c, kk_c, b_c, t_chunk)
        per_batch = lambda t: t.reshape(bn, s, c_width)
        yc = rwkv_state(per_batch(rq), per_batch(y0), per_batch(g_mat), per_batch(h_mat), t_state)
        yc = yc.reshape(n, c_width)

        xf = merge_out(xf, g_mix, w_g, ya, yb, yc, bonus_c, g_c, gate_bias[l], row2(c_lnx_g[l]),
                       row2(c_lnx_b[l]), head_sum, p_a[l].astype(BF16), p_b[l].astype(BF16),
                       p_c[l].astype(BF16), w_out[l].astype(BF16), tm)
        xf = ffn(xf, row2(norm_ffn[l]), w_gate_up[l].astype(BF16), w_down[l].astype(BF16),
                 row2(norm_final), _largest_tile(n, 1024), tf, final_norm=(l == depth - 1))
    return xf.reshape(bn, s, d)
```
